```python
import math
import jax, jax.numpy as jnp
from jax import lax
import numpy as np

D_MODEL = 4096
BATCH = 4
SEQ = 4096
DEPTH = 4

D_BRANCH = D_MODEL // 4
N_BRANCH = 3
HY_ORDER = 2
HY_CONV = 3
HY_POS_BANDS = 16
HY_POS_DIM = 1 + 2 * HY_POS_BANDS
HY_FILTER_HIDDEN = 64
HY_SLOW_DECAY = 3.07
HY_FAST_DECAY = 15.35
ML_HEADS = 8
ML_HEAD_DIM = D_BRANCH // ML_HEADS
ML_CHUNK = 128
S5_GROUP = 16
S5_GROUPS = D_BRANCH // S5_GROUP
S5_STATE = 64
R_GATE = 256
R_COND = 256
D_FF = -(-8 * D_MODEL // (3 * 256)) * 256
IN_WIDTHS = (3 * D_BRANCH, 4 * D_BRANCH, 4 * ML_HEADS, D_BRANCH, R_GATE)
IN_WIDTH = sum(IN_WIDTHS)
IN_SPLITS = tuple(sum(IN_WIDTHS[:i + 1]) for i in range(len(IN_WIDTHS) - 1))
EPS = 1e-6

kernel_name = 'hybrid_hyena_mlstm_s5_adaln_encoder'


def _rmsnorm(x, w):
    xf = x.astype(jnp.float32)
    y = xf * lax.rsqrt(jnp.mean(xf * xf, axis=-1, keepdims=True) + EPS)
    return (y * w.astype(jnp.float32)).astype(x.dtype)


def _depthwise_conv(u, w, b):
    k = w.shape[0]
    y = lax.conv_general_dilated(u, w[:, None, :].astype(u.dtype), window_strides=(1,),
                                 padding=[(k // 2, k // 2)],
                                 dimension_numbers=('NWC', 'WIO', 'NWC'),
                                 feature_group_count=u.shape[-1])
    return y + b.astype(u.dtype)


def _hyena_filters(L, w1, b1, w2, b2, w3, sin_freq, decay):
    f32 = jnp.float32
    t = jnp.arange(L, dtype=f32)
    t_unit = t / (L - 1)
    bands = jnp.linspace(1e-4, HY_POS_BANDS - 1, HY_POS_BANDS, dtype=f32)
    ang = (2.0 * math.pi / L) * t[:, None] * bands[None, :]
    z = jnp.concatenate([t_unit[:, None], jnp.cos(ang), jnp.sin(ang)], axis=-1)
    sf = sin_freq.astype(f32)
    hdn = jnp.sin(sf[0] * (z @ w1.astype(f32) + b1.astype(f32)))
    hdn = jnp.sin(sf[1] * (hdn @ w2.astype(f32) + b2.astype(f32)))
    filt = hdn @ w3.astype(f32)
    window = jnp.exp(-t_unit[:, None] * jnp.abs(decay.astype(f32))[None, :])
    return filt * window


def _long_conv(z, k_fft, L):
    zf = jnp.fft.rfft(z, n=2 * L, axis=1)
    return jnp.fft.irfft(zf * k_fft[None], n=2 * L, axis=1)[:, :L]


def _hyena(u, conv_w, conv_b, w1, b1, w2, b2, w3, sin_freq, decay, bias):
    B, L, _ = u.shape
    f32 = jnp.float32
    u3 = _depthwise_conv(u, conv_w, conv_b).astype(f32)
    x1, x2, v = jnp.split(u3, 3, axis=-1)
    filt = _hyena_filters(L, w1, b1, w2, b2, w3, sin_freq, decay).reshape(L, 2, HY_ORDER, D_BRANCH)
    k_circ = jnp.concatenate([filt[:, 0], jnp.zeros((1, HY_ORDER, D_BRANCH), f32), filt[:0:-1, 1]], axis=0)
    k_circ = k_circ * lax.rsqrt(jnp.sum(k_circ * k_circ, axis=0, keepdims=True) + EPS)
    k_fft = jnp.fft.rfft(k_circ, n=2 * L, axis=0)
    bias = bias.astype(f32)
    z = v
    for o, gate in enumerate((x1, x2)):
        z = gate * (_long_conv(z, k_fft[:, o], L) + z * bias[o])
    return z.astype(u.dtype)


def _mlstm_chunkwise(q, k, v, ig, fg):
    B, S, H, Dh = q.shape
    Lc = math.gcd(S, ML_CHUNK)
    N = S // Lc
    to_c = lambda a: a.reshape(B, N, Lc, H, Dh).transpose(0, 3, 1, 2, 4)
    to_g = lambda a: a.reshape(B, N, Lc, H).transpose(0, 3, 1, 2)
    qc, kc, vc = to_c(q), to_c(k), to_c(v)
    ic = to_g(ig)
    bcum = jnp.cumsum(to_g(jax.nn.log_sigmoid(fg)), axis=-1)
    g_tot = bcum[..., -1]
    w = g_tot[..., None] - bcum + ic
    m_loc = jnp.max(w, axis=-1)
    e_w = jnp.exp(w - m_loc[..., None])
    c_loc = jnp.einsum('bhnsk,bhnsv->bhnkv', kc * e_w[..., None], vc)
    n_loc = jnp.einsum('bhnsk,bhns->bhnk', kc, e_w)

    def step(carry, inp):
        c_st, n_st, m_st = carry
        g_n, m_l, c_l, n_l = inp
        m_new = jnp.maximum(g_n + m_st, m_l)
        a = jnp.exp(g_n + m_st - m_new)
        bb = jnp.exp(m_l - m_new)
        c_new = a[..., None, None] * c_st + bb[..., None, None] * c_l
        n_new = a[..., None] * n_st + bb[..., None] * n_l
        return (c_new, n_new, m_new), (c_st, n_st, m_st)

    init = (jnp.zeros((B, H, Dh, Dh), jnp.float32), jnp.zeros((B, H, Dh), jnp.float32),
            jnp.zeros((B, H), jnp.float32))
    xs = (jnp.moveaxis(g_tot, 2, 0), jnp.moveaxis(m_loc, 2, 0),
          jnp.moveaxis(c_loc, 2, 0), jnp.moveaxis(n_loc, 2, 0))
    _, (c_prev, n_prev, m_prev) = lax.scan(step, init, xs)
    c_prev = jnp.moveaxis(c_prev, 0, 2)
    n_prev = jnp.moveaxis(n_prev, 0, 2)
    m_prev = jnp.moveaxis(m_prev, 0, 2)
    tri = jnp.tril(jnp.ones((Lc, Lc), dtype=bool))
    dmat = bcum[..., :, None] - bcum[..., None, :] + ic[..., None, :]
    dmat = jnp.where(tri, dmat, -jnp.inf)
    inter = bcum + m_prev[..., None]
    m_t = jnp.maximum(inter, jnp.max(dmat, axis=-1))
    s = jnp.einsum('bhntk,bhnsk->bhnts', qc, kc) * jnp.exp(dmat - m_t[..., None])
    e_inter = jnp.exp(inter - m_t)
    num = jnp.einsum('bhnts,bhnsv->bhntv', s, vc) + e_inter[..., None] * jnp.einsum('bhntk,bhnkv->bhntv', qc, c_prev)
    den = jnp.sum(s, axis=-1) + e_inter * jnp.einsum('bhntk,bhnk->bhnt', qc, n_prev)
    h = num / jnp.maximum(jnp.abs(den), jnp.exp(-m_t))[..., None]
    return h.transpose(0, 2, 3, 1, 4).reshape(B, S, H, Dh)


def _mlstm(q, k, v, o, gate_pre, gate_bias, head_norm):
    B, S, _ = q.shape
    f32 = jnp.float32
    shp = (B, S, ML_HEADS, ML_HEAD_DIM)
    qh = q.astype(f32).reshape(shp)
    kh = k.astype(f32).reshape(shp) * (ML_HEAD_DIM ** -0.5)
    vh = v.astype(f32).reshape(shp)
    g = (gate_pre.astype(f32) + gate_bias.astype(f32)).reshape(B, S, 4, ML_HEADS)
    h_fwd = _mlstm_chunkwise(qh, kh, vh, g[:, :, 0], g[:, :, 1])
    rev = lambda a: jnp.flip(a, axis=1)
    h_bwd = rev(_mlstm_chunkwise(rev(qh), rev(kh), rev(vh), rev(g[:, :, 2]), rev(g[:, :, 3])))
    hs = h_fwd + h_bwd
    hs = hs * lax.rsqrt(jnp.mean(hs * hs, axis=-1, keepdims=True) + EPS)
    hs = hs.reshape(B, S, D_BRANCH) * head_norm.astype(f32)
    return (jax.nn.sigmoid(o.astype(f32)) * hs).astype(q.dtype)


def _linear_recurrence(e1, e2):
    a1, b1 = e1
    a2, b2 = e2
    return a2 * a1, a2 * b1 + b2


def _s5(u, lam_re, lam_im, log_step, b_re, b_im, c_re, c_im, d, glu_w, glu_b):
    B, S, _ = u.shape
    f32 = jnp.float32
    uf = u.astype(f32).reshape(B, S, S5_GROUPS, S5_GROUP)
    lam = lax.complex(jnp.minimum(lam_re.astype(f32), -1e-4), lam_im.astype(f32))
    step = jnp.exp(log_step.astype(f32))[..., None]
    lam_bar = jnp.exp(lam * step)
    b_t = lax.complex(b_re.astype(f32), b_im.astype(f32))
    b_bar = ((lam_bar - 1.0) / lam)[..., None] * b_t[None]
    c_t = lax.complex(c_re.astype(f32), c_im.astype(f32))

    def direction(lb, bb, flip):
        uu = jnp.flip(uf, axis=1) if flip else uf
        bu = jnp.einsum('bsgh,gph->bsgp', uu, bb)
        a = jnp.broadcast_to(lb, (1, S) + lb.shape)
        _, st = lax.associative_scan(_linear_recurrence, (a, bu), axis=1)
        return jnp.flip(st, axis=1) if flip else st

    states = direction(lam_bar[0], b_bar[0], False) + direction(lam_bar[1], b_bar[1], True)
    y = jnp.einsum('bsgp,ghp->bsgh', states, c_t).real + d.astype(f32).reshape(S5_GROUPS, S5_GROUP) * uf
    y = jax.nn.gelu(y.reshape(B, S, D_BRANCH))
    out = y * jax.nn.sigmoid(y @ glu_w.astype(f32) + glu_b.astype(f32))
    return out.astype(u.dtype)


def _hybrid_mixer(h, w_in, b_mgate, hy_conv_w, hy_conv_b, hy_f_w1, hy_f_b1, hy_f_w2, hy_f_b2,
                  hy_f_w3, hy_sin_freq, hy_decay, hy_bias, ml_norm, s5_lam_re, s5_lam_im,
                  s5_log_step, s5_b_re, s5_b_im, s5_c_re, s5_c_im, s5_d, s5_glu_w, s5_glu_b,
                  w_gate_up, b_gate, w_branch, w_out):
    proj = h @ w_in
    hy_u, qkvo, m_gates, s5_u, g_low = jnp.split(proj, IN_SPLITS, axis=-1)
    q, k, v, o = jnp.split(qkvo, 4, axis=-1)
    y_a = _hyena(hy_u, hy_conv_w, hy_conv_b, hy_f_w1, hy_f_b1, hy_f_w2, hy_f_b2, hy_f_w3,
                 hy_sin_freq, hy_decay, hy_bias)
    y_b = _mlstm(q, k, v, o, m_gates, b_mgate, ml_norm)
    y_c = _s5(s5_u, s5_lam_re, s5_lam_im, s5_log_step, s5_b_re, s5_b_im, s5_c_re, s5_c_im,
              s5_d, s5_glu_w, s5_glu_b)
    branches = (y_a, y_b, y_c)
    merged = sum(jax.nn.sigmoid(g_low @ w_gate_up[n] + b_gate[n]) * (branches[n] @ w_branch[n])
                 for n in range(N_BRANCH))
    return merged @ w_out


def _swiglu(h, wg, wu, wd):
    return (jax.nn.silu(h @ wg) * (h @ wu)) @ wd


def setup_inputs(seed: int = 0) -> dict:
    key = jax.random.key(seed)
    ks = iter(jax.random.split(key, 64))
    f32 = jnp.float32

    def nrm(shape, scale):
        return scale * jax.random.normal(next(ks), shape, f32)

    C = D_BRANCH
    F = HY_FILTER_HIDDEN
    G, P, HG = S5_GROUPS, S5_STATE, S5_GROUP
    x = nrm((BATCH, SEQ, D_MODEL), 1.0)
    c = nrm((BATCH, D_MODEL), 1.0)
    w_cond = nrm((D_MODEL, R_COND), D_MODEL ** -0.5)
    w_mod = nrm((DEPTH, R_COND, 6 * D_MODEL), 0.5 * R_COND ** -0.5)
    b_mod = nrm((DEPTH, 6 * D_MODEL), 0.02)
    norm_mix = 1.0 + nrm((DEPTH, D_MODEL), 0.02)
    norm_ffn = 1.0 + nrm((DEPTH, D_MODEL), 0.02)
    w_in = nrm((DEPTH, D_MODEL, IN_WIDTH), D_MODEL ** -0.5)
    i_bias = nrm((DEPTH, 2, 1, ML_HEADS), 0.1)
    f_bias = jnp.linspace(3.0, 6.0, ML_HEADS, dtype=f32)[None, None, None, :] + nrm((DEPTH, 2, 1, ML_HEADS), 0.1)
    b_mgate = jnp.concatenate([i_bias, f_bias], axis=2).reshape(DEPTH, 4 * ML_HEADS)
    hy_conv_w = nrm((DEPTH, HY_CONV, 3 * C), HY_CONV ** -0.5)
    hy_conv_b = nrm((DEPTH, 3 * C), 0.02)
    hy_f_w1 = nrm((DEPTH, HY_POS_DIM, F), HY_POS_DIM ** -0.5)
    hy_f_b1 = nrm((DEPTH, F), 0.02)
    hy_f_w2 = nrm((DEPTH, F, F), F ** -0.5)
    hy_f_b2 = nrm((DEPTH, F), 0.02)
    hy_f_w3 = nrm((DEPTH, F, 2 * HY_ORDER * C), F ** -0.5)
    hy_sin_freq = 1.0 + nrm((DEPTH, 2, F), 0.01)
    decay_base = jnp.broadcast_to(jnp.linspace(HY_SLOW_DECAY, HY_FAST_DECAY, C, dtype=f32), (2 * HY_ORDER, C)).reshape(-1)
    hy_decay = decay_base[None, :] + nrm((DEPTH, 2 * HY_ORDER * C), 0.05)
    hy_bias = nrm((DEPTH, HY_ORDER, C), 1.0)
    ml_norm = 1.0 + nrm((DEPTH, C), 0.02)
    s5_lam_re = -0.5 + nrm((DEPTH, 2, G, P), 0.01)
    s5_lam_im = math.pi * jnp.arange(P, dtype=f32) + nrm((DEPTH, 2, G, P), 0.01)
    s5_log_step = jax.random.uniform(next(ks), (DEPTH, 2, G), f32, math.log(1e-3), math.log(1e-1))
    s5_b_re = nrm((DEPTH, G, P, HG), (2 * HG) ** -0.5)
    s5_b_im = nrm((DEPTH, G, P, HG), (2 * HG) ** -0.5)
    s5_c_re = nrm((DEPTH, G, HG, P), P ** -0.5)
    s5_c_im = nrm((DEPTH, G, HG, P), P ** -0.5)
    s5_d = nrm((DEPTH, C), 1.0)
    s5_glu_w = nrm((DEPTH, C, C), C ** -0.5)
    s5_glu_b = nrm((DEPTH, C), 0.02)
    w_gate_up = nrm((DEPTH, N_BRANCH, R_GATE, D_MODEL), R_GATE ** -0.5)
    b_gate = nrm((DEPTH, N_BRANCH, D_MODEL), 0.02)
    w_branch = nrm((DEPTH, N_BRANCH, C, D_MODEL), C ** -0.5)
    w_out = nrm((DEPTH, D_MODEL, D_MODEL), D_MODEL ** -0.5)
    w_ffn_gate = nrm((DEPTH, D_MODEL, D_FF), D_MODEL ** -0.5)
    w_ffn_up = nrm((DEPTH, D_MODEL, D_FF), D_MODEL ** -0.5)
    w_ffn_down = nrm((DEPTH, D_FF, D_MODEL), D_FF ** -0.5)
    norm_final = 1.0 + nrm((D_MODEL,), 0.02)
    return {'x': x, 'c': c, 'w_cond': w_cond, 'w_mod': w_mod, 'b_mod': b_mod,
            'norm_mix': norm_mix, 'norm_ffn': norm_ffn, 'w_in': w_in, 'b_mgate': b_mgate,
            'hy_conv_w': hy_conv_w, 'hy_conv_b': hy_conv_b, 'hy_f_w1': hy_f_w1, 'hy_f_b1': hy_f_b1,
            'hy_f_w2': hy_f_w2, 'hy_f_b2': hy_f_b2, 'hy_f_w3': hy_f_w3, 'hy_sin_freq': hy_sin_freq,
            'hy_decay': hy_decay, 'hy_bias': hy_bias, 'ml_norm': ml_norm,
            's5_lam_re': s5_lam_re, 's5_lam_im': s5_lam_im, 's5_log_step': s5_log_step,
            's5_b_re': s5_b_re, 's5_b_im': s5_b_im, 's5_c_re': s5_c_re, 's5_c_im': s5_c_im,
            's5_d': s5_d, 's5_glu_w': s5_glu_w, 's5_glu_b': s5_glu_b,
            'w_gate_up': w_gate_up, 'b_gate': b_gate, 'w_branch': w_branch, 'w_out': w_out,
            'w_ffn_gate': w_ffn_gate, 'w_ffn_up': w_ffn_up, 'w_ffn_down': w_ffn_down,
            'norm_final': norm_final}


def reference(x, c, w_cond, w_mod, b_mod, norm_mix, norm_ffn, w_in, b_mgate, hy_conv_w, hy_conv_b,
              hy_f_w1, hy_f_b1, hy_f_w2, hy_f_b2, hy_f_w3, hy_sin_freq, hy_decay, hy_bias, ml_norm,
              s5_lam_re, s5_lam_im, s5_log_step, s5_b_re, s5_b_im, s5_c_re, s5_c_im, s5_d,
              s5_glu_w, s5_glu_b, w_gate_up, b_gate, w_branch, w_out, w_ffn_gate, w_ffn_up,
              w_ffn_down, norm_final):
    B = x.shape[0]
    cond_h = jax.nn.silu(c) @ w_cond
    for l in range(DEPTH):
        mod = (cond_h @ w_mod[l] + b_mod[l]).reshape(B, 6, 1, D_MODEL)
        shift1, scale1, gate1 = mod[:, 0], mod[:, 1], mod[:, 2]
        shift2, scale2, gate2 = mod[:, 3], mod[:, 4], mod[:, 5]
        h = _rmsnorm(x, norm_mix[l]) * (1.0 + scale1) + shift1
        x = x + gate1 * _hybrid_mixer(h, w_in[l], b_mgate[l], hy_conv_w[l], hy_conv_b[l],
                                      hy_f_w1[l], hy_f_b1[l], hy_f_w2[l], hy_f_b2[l], hy_f_w3[l],
                                      hy_sin_freq[l], hy_decay[l], hy_bias[l], ml_norm[l],
                                      s5_lam_re[l], s5_lam_im[l], s5_log_step[l], s5_b_re[l],
                                      s5_b_im[l], s5_c_re[l], s5_c_im[l], s5_d[l], s5_glu_w[l],
                                      s5_glu_b[l], w_gate_up[l], b_gate[l], w_branch[l], w_out[l])
        h = _rmsnorm(x, norm_ffn[l]) * (1.0 + scale2) + shift2
        x = x + gate2 * _swiglu(h, w_ffn_gate[l], w_ffn_up[l], w_ffn_down[l])
    return _rmsnorm(x, norm_final)
```

```python
import functools
import math

import jax
import jax.numpy as jnp
from jax import lax
from jax.experimental import pallas as pl
from jax.experimental.pallas import tpu as pltpu

F32 = jnp.float32
BF16 = jnp.bfloat16
EPS = 1e-6
HIGHEST = lax.Precision.HIGHEST

LANES = 128
HY_ORDER = 2
ML_CHUNK = 128
S5_CHUNK = 64
S5_BLOCK_GROUPS = 8
NORM_ROWS = 32
MIB = 1024 * 1024


def _tile(dim, target, align=LANES):
    if dim <= target:
        return dim
    t = (target // align) * align
    while t >= align:
        if dim % t == 0:
            return t
        t -= align
    return dim


def _params(sem, vmem_mib):
    return pltpu.CompilerParams(dimension_semantics=sem, vmem_limit_bytes=vmem_mib * MIB)


def _dot(a, b):
    return jnp.dot(a, b, preferred_element_type=F32)


def _dot_hi(a, b):
    return jnp.dot(a, b, preferred_element_type=F32, precision=HIGHEST)


def _mod_kernel(c_ref, wc_ref, wm_ref, bm_ref, o_ref):
    c = c_ref[...]
    cond_h = _dot_hi(c * jax.nn.sigmoid(c), wc_ref[...])
    o_ref[0] = _dot_hi(cond_h, wm_ref[0]) + bm_ref[0]


def _modulation(c, w_cond, w_mod, b_mod):
    B, D = c.shape
    depth, R, W = w_mod.shape
    rows = max(8, B)
    cp = jnp.zeros((rows, D), F32).at[:B].set(c)
    tn = _tile(W, 4096)
    out = pl.pallas_call(
        _mod_kernel,
        grid=(depth, W // tn),
        in_specs=[
            pl.BlockSpec((rows, D), lambda l, n: (0, 0)),
            pl.BlockSpec((D, R), lambda l, n: (0, 0)),
            pl.BlockSpec((1, R, tn), lambda l, n: (l, 0, n)),
            pl.BlockSpec((1, 1, tn), lambda l, n: (l, 0, n)),
        ],
        out_specs=pl.BlockSpec((1, rows, tn), lambda l, n: (l, 0, n)),
        out_shape=jax.ShapeDtypeStruct((depth, rows, W), F32),
        compiler_params=_params(("arbitrary", "arbitrary"), 40),
        name="adaln_modulation",
    )(cp, w_cond, w_mod, b_mod.reshape(depth, 1, W))
    return out[:, :B]


def _norm_mod(x, nw, scale, shift):
    ms = jnp.mean(x * x, axis=-1, keepdims=True)
    y = x * lax.rsqrt(ms + EPS) * nw
    return y * (1.0 + scale) + shift


def _norm_mod_rows(x_ref, nw_ref, sc_ref, sh_ref, h_scr):
    rows = min(NORM_ROWS, x_ref.shape[0])
    nw = nw_ref[...]
    scale = sc_ref[0]
    shift = sh_ref[0]

    def body(i, carry):
        r = pl.ds(pl.multiple_of(i * rows, rows), rows)
        h_scr[r, :] = _norm_mod(x_ref[r, :], nw, scale, shift).astype(BF16)
        return carry

    lax.fori_loop(0, x_ref.shape[0] // rows, body, 0)


def _inproj_kernel(x_ref, nw_ref, sc_ref, sh_ref, w_ref, ws_ref, bs_ref, o_ref, os_ref, h_scr):
    @pl.when(pl.program_id(1) == 0)
    def _():
        _norm_mod_rows(x_ref, nw_ref, sc_ref, sh_ref, h_scr)
        os_ref[...] = _dot(h_scr[...], ws_ref[...]) + bs_ref[...]

    o_ref[...] = _dot(h_scr[...], w_ref[...]).astype(BF16)


def _inproj(x2, S, nw, scale, shift, w_main, w_small, b_small):
    T, D = x2.shape
    N = w_main.shape[1]
    NS = w_small.shape[1]
    tm = _tile(S, 512)
    tn = _tile(N, 512)
    bidx = lambda m, n: ((m * tm) // S, 0, 0)
    return pl.pallas_call(
        _inproj_kernel,
        grid=(T // tm, N // tn),
        in_specs=[
            pl.BlockSpec((tm, D), lambda m, n: (m, 0)),
            pl.BlockSpec((1, D), lambda m, n: (0, 0)),
            pl.BlockSpec((1, 1, D), bidx),
            pl.BlockSpec((1, 1, D), bidx),
            pl.BlockSpec((D, tn), lambda m, n: (0, n)),
            pl.BlockSpec((D, NS), lambda m, n: (0, 0)),
            pl.BlockSpec((1, NS), lambda m, n: (0, 0)),
        ],
        out_specs=[
            pl.BlockSpec((tm, tn), lambda m, n: (m, n)),
            pl.BlockSpec((tm, NS), lambda m, n: (m, 0)),
        ],
        out_shape=[
            jax.ShapeDtypeStruct((T, N), BF16),
            jax.ShapeDtypeStruct((T, NS), F32),
        ],
        scratch_shapes=[pltpu.VMEM((tm, D), BF16)],
        compiler_params=_params(("parallel", "arbitrary"), 52),
        name="in_projection",
    )(x2, nw, scale, shift, w_main, w_small, b_small)


def _hy_pre_kernel(u_ref, w_ref, b_ref, o_ref):
    S, tn = u_ref.shape[1], u_ref.shape[2]
    R = min(256, S)
    halo = 16
    nchunks = S // R
    w = w_ref[...]
    b = b_ref[...]
    row = lax.broadcasted_iota(jnp.int32, (R, tn), 0)

    def body(i, carry):
        r0 = pl.multiple_of(i * R, R)
        x = u_ref[0, pl.ds(r0, R), :].astype(F32)
        lo = pl.multiple_of(jnp.maximum(r0 - halo, 0), halo)
        hi = pl.multiple_of(jnp.minimum(r0 + R, S - halo), halo)
        before = u_ref[0, pl.ds(lo, halo), :].astype(F32)[halo - 1:halo]
        after = u_ref[0, pl.ds(hi, halo), :].astype(F32)[0:1]
        before = jnp.where(i == 0, 0.0, before)
        after = jnp.where(i == nchunks - 1, 0.0, after)
        prev = jnp.where(row == 0, before, pltpu.roll(x, 1, 0))
        nxt = jnp.where(row == R - 1, after, pltpu.roll(x, R - 1, 0))
        y = prev * w[0:1] + x * w[1:2] + nxt * w[2:3] + b
        o_ref[0, pl.ds(r0, R), :] = y.astype(BF16)
        return carry

    lax.fori_loop(0, nchunks, body, 0)


def _hy_pre(proj3, conv_w, conv_b, C):
    B, S, _ = proj3.shape
    W = 3 * C
    tn = _tile(W, 512)
    return pl.pallas_call(
        _hy_pre_kernel,
        grid=(B, W // tn),
        in_specs=[
            pl.BlockSpec((1, S, tn), lambda b, j: (b, 0, j)),
            pl.BlockSpec((3, tn), lambda b, j: (0, j)),
            pl.BlockSpec((1, tn), lambda b, j: (0, j)),
        ],
        out_specs=pl.BlockSpec((1, S, tn), lambda b, j: (b, 0, j)),
        out_shape=jax.ShapeDtypeStruct((B, S, W), BF16),
        compiler_params=_params(("parallel", "parallel"), 48),
        name="hyena_depthwise_conv",
    )(proj3, conv_w, conv_b.reshape(1, W))


def _hy_filter_kernel(z_ref, w1_ref, b1_ref, w2_ref, b2_ref, sf_ref, w3f_ref, w3b_ref,
                      df_ref, db_ref, kp_ref, km_ref, kn_ref, ff_scr, fb_scr):
    L, tc = ff_scr.shape
    R = min(256, L)
    sf = sf_ref[...]
    dec_f = jnp.abs(df_ref[...])
    dec_b = jnp.abs(db_ref[...])
    row = lax.broadcasted_iota(jnp.int32, (R, tc), 0)
    sign = jnp.where((row & 1) == 0, 1.0, -1.0)

    def windowed(i, carry):
        ss, kn = carry
        r = pl.ds(pl.multiple_of(i * R, R), R)
        z = z_ref[r, :]
        t_unit = z[:, 0:1]
        h = jnp.sin(sf[0:1] * (_dot_hi(z, w1_ref[...]) + b1_ref[...]))
        h = jnp.sin(sf[1:2] * (_dot_hi(h, w2_ref[...]) + b2_ref[...]))
        ff = _dot_hi(h, w3f_ref[...]) * jnp.exp(-t_unit * dec_f)
        fb = _dot_hi(h, w3b_ref[...]) * jnp.exp(-t_unit * dec_b)
        fb = jnp.where(jnp.logical_and(row == 0, i == 0), 0.0, fb)
        ff_scr[r, :] = ff
        fb_scr[r, :] = fb
        ss = ss + jnp.sum(ff * ff, axis=0, keepdims=True) + jnp.sum(fb * fb, axis=0, keepdims=True)
        kn = kn + jnp.sum((ff + fb) * sign, axis=0, keepdims=True)
        return ss, kn

    zero = jnp.zeros((1, tc), F32)
    ss, kn = lax.fori_loop(0, L // R, windowed, (zero, zero))
    scale = lax.rsqrt(ss + EPS)
    kn_ref[...] = kn * scale

    def normalised(i, carry):
        r = pl.ds(pl.multiple_of(i * R, R), R)
        ff = ff_scr[r, :]
        fb = fb_scr[r, :]
        kp_ref[r, :] = ((ff + fb) * scale).astype(BF16)
        km_ref[r, :] = ((fb - ff) * scale).astype(BF16)
        return carry

    lax.fori_loop(0, L // R, normalised, 0)


def _hy_filters(zpos, w1, b1, w2, b2, sf, w3, decay, C):
    L, E = zpos.shape
    F = w1.shape[1]
    tc = _tile(C, 256)
    nt = C // tc
    fwd = lambda o, j: (0, o * nt + j)
    bwd = lambda o, j: (0, (HY_ORDER + o) * nt + j)
    const = lambda o, j: (0, 0)
    return pl.pallas_call(
        _hy_filter_kernel,
        grid=(HY_ORDER, nt),
        in_specs=[
            pl.BlockSpec((L, E), const),
            pl.BlockSpec((E, F), const),
            pl.BlockSpec((1, F), const),
            pl.BlockSpec((F, F), const),
            pl.BlockSpec((1, F), const),
            pl.BlockSpec((2, F), const),
            pl.BlockSpec((F, tc), fwd),
            pl.BlockSpec((F, tc), bwd),
            pl.BlockSpec((1, tc), fwd),
            pl.BlockSpec((1, tc), bwd),
        ],
        out_specs=[
            pl.BlockSpec((L, tc), fwd),
            pl.BlockSpec((L, tc), fwd),
            pl.BlockSpec((1, tc), fwd),
        ],
        out_shape=[
            jax.ShapeDtypeStruct((L, HY_ORDER * C), BF16),
            jax.ShapeDtypeStruct((L, HY_ORDER * C), BF16),
            jax.ShapeDtypeStruct((1, HY_ORDER * C), F32),
        ],
        scratch_shapes=[pltpu.VMEM((L, tc), F32), pltpu.VMEM((L, tc), F32)],
        compiler_params=_params(("parallel", "parallel"), 40),
        name="hyena_filters",
    )(zpos, w1, b1, w2, b2, sf, w3, w3, decay, decay)


def _first_bin_mask(shape):
    row = lax.broadcasted_iota(jnp.int32, shape, 0)
    return jnp.logical_and(row == 0, pl.program_id(0) == 0)


def _spec_filter_kernel(cm_ref, sm_ref, kp_ref, km_ref, kn_ref, kr_ref, ki_ref):
    kr_ref[...] = _dot(cm_ref[...], kp_ref[...])
    ki = _dot(sm_ref[...], km_ref[...])
    ki_ref[...] = jnp.where(_first_bin_mask(ki.shape), kn_ref[...], ki)


def _hy_filter_spectrum(cm, sm, kp, km, kn):
    L, W = kp.shape
    tk = _tile(L, 512)
    tn = _tile(W, 512)
    return pl.pallas_call(
        _spec_filter_kernel,
        grid=(L // tk, W // tn),
        in_specs=[
            pl.BlockSpec((tk, L), lambda k, j: (k, 0)),
            pl.BlockSpec((tk, L), lambda k, j: (k, 0)),
            pl.BlockSpec((L, tn), lambda k, j: (0, j)),
            pl.BlockSpec((L, tn), lambda k, j: (0, j)),
            pl.BlockSpec((1, tn), lambda k, j: (0, j)),
        ],
        out_specs=[
            pl.BlockSpec((tk, tn), lambda k, j: (k, j)),
            pl.BlockSpec((tk, tn), lambda k, j: (k, j)),
        ],
        out_shape=[jax.ShapeDtypeStruct((L, W), F32)] * 2,
        compiler_params=_params(("parallel", "parallel"), 48),
        name="hyena_filter_spectrum",
    )(cm, sm, kp, km, kn)


def _spec_data_kernel(cm_ref, sm_ref, z_ref, kr_ref, ki_ref, pr_ref, pi_ref):
    z = z_ref[0]
    zr = _dot(cm_ref[...], z)
    zs = _dot(sm_ref[...], z)
    kr = kr_ref[...]
    ki = ki_ref[...]
    m0 = _first_bin_mask(zr.shape)
    zski = zs * ki
    pr_ref[0] = (zr * kr + jnp.where(m0, 0.0, zski)).astype(BF16)
    pi_ref[0] = jnp.where(m0, zski, zr * ki - zs * kr).astype(BF16)


def _hy_spectrum_product(cm, sm, zarr, zcol, kr, ki, order, C):
    B, L, _ = zarr.shape
    tk = _tile(L, 512)
    tn = _tile(C, 512)
    nt = C // tn
    zoff = zcol // tn
    return pl.pallas_call(
        _spec_data_kernel,
        grid=(L // tk, B, nt),
        in_specs=[
            pl.BlockSpec((tk, L), lambda k, b, j: (k, 0)),
            pl.BlockSpec((tk, L), lambda k, b, j: (k, 0)),
            pl.BlockSpec((1, L, tn), lambda k, b, j: (b, 0, zoff + j)),
            pl.BlockSpec((tk, tn), lambda k, b, j: (k, order * nt + j)),
            pl.BlockSpec((tk, tn), lambda k, b, j: (k, order * nt + j)),
        ],
        out_specs=[
            pl.BlockSpec((1, tk, tn), lambda k, b, j: (b, k, j)),
            pl.BlockSpec((1, tk, tn), lambda k, b, j: (b, k, j)),
        ],
        out_shape=[jax.ShapeDtypeStruct((B, L, C), BF16)] * 2,
        compiler_params=_params(("parallel", "parallel", "parallel"), 48),
        name="hyena_spectrum_product",
    )(cm, sm, zarr, kr, ki)


def _inv_kernel(icm_ref, ism_ref, pr_ref, pi_ref, g_ref, z_ref, b_ref, o_ref):
    y = _dot(icm_ref[...], pr_ref[0]) + _dot(ism_ref[...], pi_ref[0])
    z = z_ref[0].astype(F32)
    o_ref[0] = (g_ref[0].astype(F32) * (y + z * b_ref[0])).astype(BF16)


def _hy_inverse_gate(icm, ism, pr, pi, u3, zarr, zcol, bias, order, C):
    B, L, _ = pr.shape
    tt = _tile(L, 512)
    tn = _tile(C, 512)
    nt = C // tn
    zoff = zcol // tn
    return pl.pallas_call(
        _inv_kernel,
        grid=(L // tt, B, nt),
        in_specs=[
            pl.BlockSpec((tt, L), lambda t, b, j: (t, 0)),
            pl.BlockSpec((tt, L), lambda t, b, j: (t, 0)),
            pl.BlockSpec((1, L, tn), lambda t, b, j: (b, 0, j)),
            pl.BlockSpec((1, L, tn), lambda t, b, j: (b, 0, j)),
            pl.BlockSpec((1, tt, tn), lambda t, b, j: (b, t, order * nt + j)),
            pl.BlockSpec((1, tt, tn), lambda t, b, j: (b, t, zoff + j)),
            pl.BlockSpec((1, 1, tn), lambda t, b, j: (order, 0, j)),
        ],
        out_specs=pl.BlockSpec((1, tt, tn), lambda t, b, j: (b, t, j)),
        out_shape=jax.ShapeDtypeStruct((B, L, C), BF16),
        compiler_params=_params(("parallel", "parallel", "parallel"), 48),
        name="hyena_inverse_gate",
    )(icm, ism, pr, pi, u3, zarr, bias)


def _dft_matrices(L):
    N = 2 * L
    k = jnp.arange(L, dtype=jnp.int32)
    ang = ((k[:, None] * k[None, :]) % N).astype(F32) * (2.0 * math.pi / N)
    cm = jnp.cos(ang)
    sm = jnp.sin(ang)
    alt = jnp.where(k % 2 == 0, 1.0, -1.0).astype(F32)
    smf = sm.at[0, :].set(alt)
    icm = (cm * (2.0 / N)).at[:, 0].set(1.0 / N)
    ism = (sm * (-2.0 / N)).at[:, 0].set(alt / N)
    return cm.astype(BF16), smf.astype(BF16), icm.astype(BF16), ism.astype(BF16)


def _hy_positions(L, bands, width):
    t = jnp.arange(L, dtype=F32)
    t_unit = t / (L - 1)
    freqs = jnp.linspace(1e-4, bands - 1, bands, dtype=F32)
    ang = (2.0 * math.pi / L) * t[:, None] * freqs[None, :]
    z = jnp.concatenate([t_unit[:, None], jnp.cos(ang), jnp.sin(ang)], axis=-1)
    return jnp.pad(z, ((0, 0), (0, width - z.shape[1])))


def _log_sigmoid(x):
    return jnp.minimum(x, 0.0) - jnp.log(1.0 + jnp.exp(-jnp.abs(x)))


def _cumsum_lanes(x):
    lane = lax.broadcasted_iota(jnp.int32, x.shape, 1)
    sh = 1
    while sh < x.shape[1]:
        x = x + jnp.where(lane >= sh, pltpu.roll(x, sh, 1), 0.0)
        sh *= 2
    return x


def _mlstm_kernel(q_ref, k_ref, v_ref, o_ref, g_ref, hn_ref, y_ref, hf_scr, bc_scr, *, heads):
    S, Dh = hf_scr.shape
    Lc = ML_CHUNK
    NC = S // Lc
    hd = pl.program_id(1)
    kscale = Dh ** -0.5
    ri = lax.broadcasted_iota(jnp.int32, (Lc, Lc), 0)
    ci = lax.broadcasted_iota(jnp.int32, (Lc, Lc), 1)

    def run(direction, finish):
        ig_rows = g_ref.at[0, (2 * direction) * heads + hd]
        logf = _log_sigmoid(g_ref[0, (2 * direction + 1) * heads + hd])
        pre = _cumsum_lanes(logf)
        if direction == 0:
            bc_scr[...] = pre
            tri = ri >= ci
        else:
            bc_scr[...] = jnp.sum(logf, axis=1, keepdims=True) - pre + logf
            tri = ri <= ci

        def body(i, carry):
            c_st, n_st, m_st = carry
            n = i if direction == 0 else NC - 1 - i
            rows = pl.ds(pl.multiple_of(n * Lc, Lc), Lc)
            q = q_ref[0, rows, :]
            k = k_ref[0, rows, :]
            v = v_ref[0, rows, :]
            bc_r = bc_scr[pl.ds(n, 1), :]
            ig_r = ig_rows[pl.ds(n, 1), :]
            g_tot = bc_r[:, Lc - 1:Lc] if direction == 0 else bc_r[:, 0:1]
            stacked = jnp.where(ri == 0, bc_r, jnp.where(ri == 1, ig_r, 0.0))
            st_t = stacked.T
            bc_c = st_t[:, 0:1]
            ig_c = st_t[:, 1:2]
            w_c = g_tot - bc_c + ig_c
            m_loc = jnp.max(w_c, axis=0, keepdims=True)
            e_w = jnp.exp(w_c - m_loc) * kscale
            kw = k.astype(F32) * e_w
            c_loc = lax.dot_general(kw.astype(BF16), v, (((0,), (0,)), ((), ())),
                                    preferred_element_type=F32)
            n_loc = jnp.sum(kw, axis=0, keepdims=True)
            dmat = jnp.where(tri, bc_c - bc_r + ig_r, -jnp.inf)
            inter = bc_c + m_st
            m_t = jnp.maximum(inter, jnp.max(dmat, axis=1, keepdims=True))
            qk = lax.dot_general(q, k, (((1,), (1,)), ((), ())), preferred_element_type=F32)
            s = qk * (jnp.exp(dmat - m_t) * kscale)
            e_inter = jnp.exp(inter - m_t)
            num = _dot(s.astype(BF16), v) + e_inter * _dot(q, c_st.astype(BF16))
            den = (jnp.sum(s, axis=1, keepdims=True)
                   + e_inter * jnp.sum(q.astype(F32) * n_st, axis=1, keepdims=True))
            h = num / jnp.maximum(jnp.abs(den), jnp.exp(-m_t))
            finish(rows, h)
            m_new = jnp.maximum(g_tot + m_st, m_loc)
            a = jnp.exp(g_tot + m_st - m_new)
            bb = jnp.exp(m_loc - m_new)
            return a * c_st + bb * c_loc, a * n_st + bb * n_loc, m_new

        init = (jnp.zeros((Dh, Dh), F32), jnp.zeros((1, Dh), F32), jnp.zeros((1, 1), F32))
        lax.fori_loop(0, NC, body, init)

    def store_fwd(rows, h):
        hf_scr[rows, :] = h

    def store_out(rows, h):
        hs = hf_scr[rows, :] + h
        hs = hs * lax.rsqrt(jnp.mean(hs * hs, axis=-1, keepdims=True) + EPS) * hn_ref[...]
        y_ref[0, rows, :] = (jax.nn.sigmoid(o_ref[0, rows, :].astype(F32)) * hs).astype(BF16)

    run(0, store_fwd)
    run(1, store_out)


def _mlstm(proj3, gate_rows, head_norm, C, heads):
    B, S, _ = proj3.shape
    Dh = C // heads
    NC = S // ML_CHUNK
    base = 3 * C // Dh
    col = lambda off: (lambda b, h: (b, 0, base + off * heads + h))
    return pl.pallas_call(
        functools.partial(_mlstm_kernel, heads=heads),
        grid=(B, heads),
        in_specs=[
            pl.BlockSpec((1, S, Dh), col(0)),
            pl.BlockSpec((1, S, Dh), col(1)),
            pl.BlockSpec((1, S, Dh), col(2)),
            pl.BlockSpec((1, S, Dh), col(3)),
            pl.BlockSpec((1, 4 * heads, NC, ML_CHUNK), lambda b, h: (b, 0, 0, 0)),
            pl.BlockSpec((1, Dh), lambda b, h: (0, h)),
        ],
        out_specs=pl.BlockSpec((1, S, Dh), lambda b, h: (b, 0, h)),
        out_shape=jax.ShapeDtypeStruct((B, S, C), BF16),
        scratch_shapes=[pltpu.VMEM((S, Dh), F32), pltpu.VMEM((NC, ML_CHUNK), F32)],
        compiler_params=_params(("parallel", "parallel"), 32),
        name="mlstm",
    )(proj3, proj3, proj3, proj3, gate_rows, head_norm)


def _cmul(ar, ai, br, bi):
    return ar * br - ai * bi, ar * bi + ai * br


def _s5_kernel(u_ref, w_ref, cm_ref, e_ref, dm_ref, lam_ref, d_ref, y_ref, acc_scr):
    S = acc_scr.shape[0]
    T = e_ref.shape[2]
    P2 = e_ref.shape[3]
    Ph = P2 // 2
    NCH = S // T
    ri = lax.broadcasted_iota(jnp.int32, (T, T), 0)
    ci = lax.broadcasted_iota(jnp.int32, (T, T), 1)

    for direction in (0, 1):
        tri = (ri >= ci) if direction == 0 else (ri <= ci)
        tri = jnp.where(tri, 1.0, 0.0).astype(BF16)
        e = e_ref[direction, 0]
        dm = dm_ref[direction, 0]
        lam = lam_ref[direction, 0]
        er, ei = e[:, :Ph], e[:, Ph:]
        dr, di = dm[:, :Ph], dm[:, Ph:]
        lr, li = lam[:, :Ph], lam[:, Ph:]
        w = w_ref[direction, 0]
        last = T - 1 if direction == 0 else 0

        def body(i, carry):
            cr, ci_ = carry
            n = i if direction == 0 else NCH - 1 - i
            rows = pl.ds(pl.multiple_of(n * T, T), T)
            x = _dot(u_ref[0, rows, :], w)
            sr, si = _cmul(er, ei, x[:, :Ph], x[:, Ph:])
            cs = _dot(tri, jnp.concatenate([sr, si], axis=1).astype(BF16))
            br, bi = _cmul(lr, li, cr, ci_)
            hr, hi = _cmul(dr, di, cs[:, :Ph] + br, cs[:, Ph:] + bi)
            y = _dot(jnp.concatenate([hr, hi], axis=1).astype(BF16), cm_ref[0])
            if direction == 0:
                acc_scr[rows, :] = y
            else:
                acc_scr[rows, :] += y
            return hr[last:last + 1, :], hi[last:last + 1, :]

        zero = jnp.zeros((1, Ph), F32)
        lax.fori_loop(0, NCH, body, (zero, zero))

    R = min(512, S)
    d = d_ref[...]

    def skip_gelu(i, carry):
        r = pl.ds(pl.multiple_of(i * R, R), R)
        y = acc_scr[r, :] + d * u_ref[0, r, :].astype(F32)
        y_ref[0, r, :] = jax.nn.gelu(y).astype(BF16)
        return carry

    lax.fori_loop(0, S // R, skip_gelu, 0)


def _s5(proj3, w_bd, c_bd, e_tab, d_tab, lam_tab, d_skip, C):
    B, S, _ = proj3.shape
    NB, P2, U = c_bd.shape
    T = e_tab.shape[2]
    base = 7 * C // U
    return pl.pallas_call(
        _s5_kernel,
        grid=(B, NB),
        in_specs=[
            pl.BlockSpec((1, S, U), lambda b, j: (b, 0, base + j)),
            pl.BlockSpec((2, 1, U, P2), lambda b, j: (0, j, 0, 0)),
            pl.BlockSpec((1, P2, U), lambda b, j: (j, 0, 0)),
            pl.BlockSpec((2, 1, T, P2), lambda b, j: (0, j, 0, 0)),
            pl.BlockSpec((2, 1, T, P2), lambda b, j: (0, j, 0, 0)),
            pl.BlockSpec((2, 1, 1, P2), lambda b, j: (0, j, 0, 0)),
            pl.BlockSpec((1, U), lambda b, j: (0, j)),
        ],
        out_specs=pl.BlockSpec((1, S, U), lambda b, j: (b, 0, j)),
        out_shape=jax.ShapeDtypeStruct((B, S, C), BF16),
        scratch_shapes=[pltpu.VMEM((S, U), F32)],
        compiler_params=_params(("parallel", "parallel"), 32),
        name="s5_scan",
    )(proj3, w_bd, c_bd, e_tab, d_tab, lam_tab, d_skip)


def _s5_tables(lam_re, lam_im, log_step, b_re, b_im, c_re, c_im):
    _, G, P = lam_re.shape
    Hg = b_re.shape[-1]
    gb = S5_BLOCK_GROUPS
    NB = G // gb
    T = S5_CHUNK
    lre = jnp.minimum(lam_re, -1e-4)
    lim = lam_im
    step = jnp.exp(log_step)[..., None]
    mag = jnp.exp(lre * step)
    ang = lim * step
    lbr = mag * jnp.cos(ang)
    lbi = mag * jnp.sin(ang)
    den = lre * lre + lim * lim
    qr = ((lbr - 1.0) * lre + lbi * lim) / den
    qi = (lbi * lre - (lbr - 1.0) * lim) / den
    bbr = qr[..., None] * b_re[None] - qi[..., None] * b_im[None]
    bbi = qr[..., None] * b_im[None] + qi[..., None] * b_re[None]
    eye = jnp.eye(gb, dtype=F32)

    def drive(bb):
        t = bb.reshape(2, NB, gb, P, Hg).transpose(0, 1, 2, 4, 3)
        return jnp.einsum("ab,djahp->djahbp", eye, t).reshape(2, NB, gb * Hg, gb * P)

    w_bd = jnp.concatenate([drive(bbr), drive(bbi)], axis=-1).astype(BF16)

    def readout(cc):
        t = cc.reshape(NB, gb, Hg, P).transpose(0, 1, 3, 2)
        return jnp.einsum("ab,japh->japbh", eye, t).reshape(NB, gb * P, gb * Hg)

    c_bd = jnp.concatenate([readout(c_re), -readout(c_im)], axis=1).astype(BF16)

    def blocks(a):
        return a.reshape(2, -1, NB, gb * P).transpose(0, 2, 1, 3)

    r = jnp.arange(T, dtype=F32)
    r = jnp.stack([r, T - 1.0 - r])[:, :, None, None]
    lm = (lre * step)[:, None]
    an = ang[:, None]
    e_tab = jnp.concatenate([blocks(jnp.exp(-r * lm) * jnp.cos(-r * an)),
                             blocks(jnp.exp(-r * lm) * jnp.sin(-r * an))], axis=-1)
    d_tab = jnp.concatenate([blocks(jnp.exp(r * lm) * jnp.cos(r * an)),
                             blocks(jnp.exp(r * lm) * jnp.sin(r * an))], axis=-1)
    lam_tab = jnp.concatenate([blocks(lbr[:, None]), blocks(lbi[:, None])], axis=-1)
    return w_bd, c_bd, e_tab, d_tab, lam_tab


def _merge_kernel(ya_ref, yb_ref, yc_ref, gl_ref, gw_ref, gb_ref, wg_ref, bg_ref, wb_ref,
                  o_ref, yc_scr):
    @pl.when(pl.program_id(1) == 0)
    def _():
        y = yc_ref[...]
        gate = jax.nn.sigmoid(_dot(y, gw_ref[...]) + gb_ref[...])
        yc_scr[...] = (y.astype(F32) * gate).astype(BF16)

    gl = gl_ref[...].astype(BF16)
    acc = None
    for n, br in enumerate((ya_ref[...], yb_ref[...], yc_scr[...])):
        term = jax.nn.sigmoid(_dot(gl, wg_ref[n]) + bg_ref[n]) * _dot(br, wb_ref[n])
        acc = term if acc is None else acc + term
    o_ref[...] = acc.astype(BF16)


def _merge(ya, yb, yc, small, glu_w, glu_b, wg, bg, wb):
    T, C = ya.shape
    NBR, R, D = wg.shape
    tm = _tile(T, 512)
    tn = _tile(D, 512)
    row = lambda m, n: (m, 0)
    return pl.pallas_call(
        _merge_kernel,
        grid=(T // tm, D // tn),
        in_specs=[
            pl.BlockSpec((tm, C), row),
            pl.BlockSpec((tm, C), row),
            pl.BlockSpec((tm, C), row),
            pl.BlockSpec((tm, R), row),
            pl.BlockSpec((C, C), lambda m, n: (0, 0)),
            pl.BlockSpec((1, C), lambda m, n: (0, 0)),
            pl.BlockSpec((NBR, R, tn), lambda m, n: (0, 0, n)),
            pl.BlockSpec((NBR, 1, tn), lambda m, n: (0, 0, n)),
            pl.BlockSpec((NBR, C, tn), lambda m, n: (0, 0, n)),
        ],
        out_specs=pl.BlockSpec((tm, tn), lambda m, n: (m, n)),
        out_shape=jax.ShapeDtypeStruct((T, D), BF16),
        scratch_shapes=[pltpu.VMEM((tm, C), BF16)],
        compiler_params=_params(("parallel", "arbitrary"), 40),
        name="branch_merge",
    )(ya, yb, yc, small, glu_w, glu_b, wg, bg, wb)


def _proj_res_kernel(a_ref, w_ref, x_ref, g_ref, o_ref):
    o_ref[...] = x_ref[...] + g_ref[0] * _dot(a_ref[...], w_ref[...])


def _proj_residual(a, w, x2, gate, S, tm_target, tn_target, name):
    T, K = a.shape
    D = w.shape[1]
    tm = _tile(S, tm_target)
    tn = _tile(D, tn_target)
    return pl.pallas_call(
        _proj_res_kernel,
        grid=(T // tm, D // tn),
        in_specs=[
            pl.BlockSpec((tm, K), lambda m, n: (m, 0)),
            pl.BlockSpec((K, tn), lambda m, n: (0, n)),
            pl.BlockSpec((tm, tn), lambda m, n: (m, n)),
            pl.BlockSpec((1, 1, tn), lambda m, n: ((m * tm) // S, 0, n)),
        ],
        out_specs=pl.BlockSpec((tm, tn), lambda m, n: (m, n)),
        out_shape=jax.ShapeDtypeStruct((T, D), F32),
        compiler_params=_params(("parallel", "parallel"), 52),
        name=name,
    )(a, w, x2, gate)


def _ffn_up_kernel(x_ref, nw_ref, sc_ref, sh_ref, wg_ref, wu_ref, o_ref, h_scr):
    @pl.when(pl.program_id(1) == 0)
    def _():
        _norm_mod_rows(x_ref, nw_ref, sc_ref, sh_ref, h_scr)

    h = h_scr[...]
    g = _dot(h, wg_ref[...])
    o_ref[...] = (g * jax.nn.sigmoid(g) * _dot(h, wu_ref[...])).astype(BF16)


def _ffn_up(x2, S, nw, scale, shift, wg, wu):
    T, D = x2.shape
    N = wg.shape[1]
    tm = _tile(S, 512)
    tn = _tile(N, 512)
    bidx = lambda m, n: ((m * tm) // S, 0, 0)
    return pl.pallas_call(
        _ffn_up_kernel,
        grid=(T // tm, N // tn),
        in_specs=[
            pl.BlockSpec((tm, D), lambda m, n: (m, 0)),
            pl.BlockSpec((1, D), lambda m, n: (0, 0)),
            pl.BlockSpec((1, 1, D), bidx),
            pl.BlockSpec((1, 1, D), bidx),
            pl.BlockSpec((D, tn), lambda m, n: (0, n)),
            pl.BlockSpec((D, tn), lambda m, n: (0, n)),
        ],
        out_specs=pl.BlockSpec((tm, tn), lambda m, n: (m, n)),
        out_shape=jax.ShapeDtypeStruct((T, N), BF16),
        scratch_shapes=[pltpu.VMEM((tm, D), BF16)],
        compiler_params=_params(("parallel", "arbitrary"), 52),
        name="swiglu_up",
    )(x2, nw, scale, shift, wg, wu)


def _final_norm_kernel(x_ref, w_ref, o_ref):
    rows = min(NORM_ROWS, x_ref.shape[0])
    w = w_ref[...]

    def body(i, carry):
        r = pl.ds(pl.multiple_of(i * rows, rows), rows)
        x = x_ref[r, :]
        o_ref[r, :] = x * lax.rsqrt(jnp.mean(x * x, axis=-1, keepdims=True) + EPS) * w
        return carry

    lax.fori_loop(0, x_ref.shape[0] // rows, body, 0)


def _final_norm(x2, w):
    T, D = x2.shape
    tm = _tile(T, 256)
    return pl.pallas_call(
        _final_norm_kernel,
        grid=(T // tm,),
        in_specs=[pl.BlockSpec((tm, D), lambda m: (m, 0)), pl.BlockSpec((1, D), lambda m: (0, 0))],
        out_specs=pl.BlockSpec((tm, D), lambda m: (m, 0)),
        out_shape=jax.ShapeDtypeStruct((T, D), F32),
        compiler_params=_params(("parallel",), 40),
        name="final_norm",
    )(x2, w)


def _pad_to(a, axis, size):
    pad = [(0, 0)] * a.ndim
    pad[axis] = (0, size - a.shape[axis])
    return jnp.pad(a, pad)


def kernel(x, c, w_cond, w_mod, b_mod, norm_mix, norm_ffn, w_in, b_mgate, hy_conv_w, hy_conv_b, hy_f_w1, hy_f_b1, hy_f_w2, hy_f_b2, hy_f_w3, hy_sin_freq, hy_decay, hy_bias, ml_norm, s5_lam_re, s5_lam_im, s5_log_step, s5_b_re, s5_b_im, s5_c_re, s5_c_im, s5_d, s5_glu_w, s5_glu_b, w_gate_up, b_gate, w_branch, w_out, w_ffn_gate, w_ffn_up, w_ffn_down, norm_final):
    B, S, D = x.shape
    depth = w_in.shape[0]
    C = D // 4
    T = B * S
    n_gate = b_mgate.shape[-1]
    heads = n_gate // 4
    R = w_gate_up.shape[2]
    NS = -(-(R + n_gate) // LANES) * LANES
    NC = S // ML_CHUNK
    d_ff = w_ffn_gate.shape[-1]
    d_ff_pad = -(-d_ff // 512) * 512
    bands = (hy_f_w1.shape[1] - 1) // 2
    F = hy_f_w1.shape[2]
    Fp = -(-F // LANES) * LANES

    mod = _modulation(c, w_cond, w_mod, b_mod).reshape(depth, B, 6, 1, D)
    cm, sm, icm, ism = _dft_matrices(S)
    zpos = _hy_positions(S, bands, LANES)

    x2 = x.reshape(T, D)
    for l in range(depth):
        shift1, scale1, gate1, shift2, scale2, gate2 = (mod[l, :, i] for i in range(6))

        wl = w_in[l]
        w_main = jnp.concatenate([wl[:, :7 * C], wl[:, 7 * C + n_gate:8 * C + n_gate]], axis=1)
        w_small = jnp.concatenate([wl[:, 8 * C + n_gate:], wl[:, 7 * C:7 * C + n_gate]], axis=1)
        w_small = _pad_to(w_small, 1, NS)
        b_small = _pad_to(jnp.concatenate([jnp.zeros((R,), F32), b_mgate[l]]), 0, NS).reshape(1, NS)
        proj, small = _inproj(x2, S, norm_mix[l].reshape(1, D), scale1, shift1,
                              w_main.astype(BF16), w_small.astype(BF16), b_small)
        proj3 = proj.reshape(B, S, 8 * C)

        u3 = _hy_pre(proj3, hy_conv_w[l], hy_conv_b[l], C)
        kp, km, kn = _hy_filters(
            zpos, _pad_to(_pad_to(hy_f_w1[l], 0, LANES), 1, Fp), _pad_to(hy_f_b1[l], 0, Fp).reshape(1, Fp),
            _pad_to(_pad_to(hy_f_w2[l], 0, Fp), 1, Fp), _pad_to(hy_f_b2[l], 0, Fp).reshape(1, Fp),
            _pad_to(hy_sin_freq[l], 1, Fp), _pad_to(hy_f_w3[l], 0, Fp),
            hy_decay[l].reshape(1, -1), C)
        kr, ki = _hy_filter_spectrum(cm, sm, kp, km, kn)
        hy_b = hy_bias[l].reshape(HY_ORDER, 1, C)
        zarr, zcol = u3, 2 * C
        for o in range(HY_ORDER):
            pr, pi = _hy_spectrum_product(cm, sm, zarr, zcol, kr, ki, o, C)
            zarr = _hy_inverse_gate(icm, ism, pr, pi, u3, zarr, zcol, hy_b, o, C)
            zcol = 0
        y_a = zarr.reshape(T, C)

        gate_rows = small[:, R:R + n_gate].reshape(B, NC, ML_CHUNK, n_gate).transpose(0, 3, 1, 2)
        y_b = _mlstm(proj3, gate_rows, ml_norm[l].reshape(1, C), C, heads).reshape(T, C)

        w_bd, c_bd, e_tab, d_tab, lam_tab = _s5_tables(
            s5_lam_re[l], s5_lam_im[l], s5_log_step[l], s5_b_re[l], s5_b_im[l], s5_c_re[l], s5_c_im[l])
        y_c = _s5(proj3, w_bd, c_bd, e_tab, d_tab, lam_tab, s5_d[l].reshape(1, C), C).reshape(T, C)

        merged = _merge(y_a, y_b, y_c, small, s5_glu_w[l].astype(BF16), s5_glu_b[l].reshape(1, C),
                        w_gate_up[l].astype(BF16), b_gate[l].reshape(-1, 1, D), w_branch[l].astype(BF16))
        x2 = _proj_residual(merged, w_out[l].astype(BF16), x2, gate1, S, 512, 512, "out_projection")

        wg = _pad_to(w_ffn_gate[l], 1, d_ff_pad).astype(BF16)
        wu = _pad_to(w_ffn_up[l], 1, d_ff_pad).astype(BF16)
        wd = _pad_to(w_ffn_down[l], 0, d_ff_pad).astype(BF16)
        hidden = _ffn_up(x2, S, norm_ffn[l].reshape(1, D), scale2, shift2, wg, wu)
        x2 = _proj_residual(hidden, wd, x2, gate2, S, 512, 256, "swiglu_down")

    return _final_norm(x2, norm_final.reshape(1, D)).reshape(B, S, D)
```

```python
import functools
import math

import jax
import jax.numpy as jnp
from jax import lax
from jax.experimental import pallas as pl
from jax.experimental.pallas import tpu as pltpu

F32 = jnp.float32
BF16 = jnp.bfloat16
EPS = 1e-6
HIGHEST = lax.Precision.HIGHEST

LANES = 128
HY_ORDER = 2
ML_CHUNK = 128
S5_CHUNK = 64
S5_BLOCK_GROUPS = 8
NORM_ROWS = 32
TOKEN_TILE = 1024
MIB = 1024 * 1024


def _tile(dim, target, align=LANES):
    if dim <= target:
        return dim
    t = (target // align) * align
    while t >= align:
        if dim % t == 0:
            return t
        t -= align
    return dim


def _params(sem, vmem_mib):
    return pltpu.CompilerParams(dimension_semantics=sem, vmem_limit_bytes=vmem_mib * MIB)


def _dot(a, b):
    return jnp.dot(a, b, preferred_element_type=F32)


def _dot_hi(a, b):
    return jnp.dot(a, b, preferred_element_type=F32, precision=HIGHEST)


def _mod_kernel(c_ref, wc_ref, wm_ref, bm_ref, o_ref):
    c = c_ref[...]
    cond_h = _dot_hi(c * jax.nn.sigmoid(c), wc_ref[...])
    o_ref[0] = _dot_hi(cond_h, wm_ref[0]) + bm_ref[0]


def _modulation(c, w_cond, w_mod, b_mod):
    B, D = c.shape
    depth, R, W = w_mod.shape
    rows = max(8, B)
    cp = jnp.zeros((rows, D), F32).at[:B].set(c)
    tn = _tile(W, 4096)
    out = pl.pallas_call(
        _mod_kernel,
        grid=(depth, W // tn),
        in_specs=[
            pl.BlockSpec((rows, D), lambda l, n: (0, 0)),
            pl.BlockSpec((D, R), lambda l, n: (0, 0)),
            pl.BlockSpec((1, R, tn), lambda l, n: (l, 0, n)),
            pl.BlockSpec((1, 1, tn), lambda l, n: (l, 0, n)),
        ],
        out_specs=pl.BlockSpec((1, rows, tn), lambda l, n: (l, 0, n)),
        out_shape=jax.ShapeDtypeStruct((depth, rows, W), F32),
        compiler_params=_params(("arbitrary", "arbitrary"), 40),
        name="adaln_modulation",
    )(cp, w_cond, w_mod, b_mod.reshape(depth, 1, W))
    return out[:, :B]


def _norm_mod(x, nw, scale, shift):
    ms = jnp.mean(x * x, axis=-1, keepdims=True)
    y = x * lax.rsqrt(ms + EPS) * nw
    return y * (1.0 + scale) + shift


def _norm_mod_rows(x_ref, nw_ref, sc_ref, sh_ref, h_scr):
    rows = min(NORM_ROWS, x_ref.shape[0])
    nw = nw_ref[...]
    scale = sc_ref[0]
    shift = sh_ref[0]

    def body(i, carry):
        r = pl.ds(pl.multiple_of(i * rows, rows), rows)
        h_scr[r, :] = _norm_mod(x_ref[r, :], nw, scale, shift).astype(BF16)
        return carry

    lax.fori_loop(0, x_ref.shape[0] // rows, body, 0)


def _inproj_kernel(x_ref, nw_ref, sc_ref, sh_ref, w_ref, ws_ref, bs_ref, o_ref, os_ref, h_scr):
    @pl.when(pl.program_id(1) == 0)
    def _():
        _norm_mod_rows(x_ref, nw_ref, sc_ref, sh_ref, h_scr)
        os_ref[...] = _dot(h_scr[...], ws_ref[...]) + bs_ref[...]

    o_ref[...] = _dot(h_scr[...], w_ref[...]).astype(BF16)


def _inproj(x2, S, nw, scale, shift, w_main, w_small, b_small):
    T, D = x2.shape
    N = w_main.shape[1]
    NS = w_small.shape[1]
    tm = _tile(S, TOKEN_TILE)
    tn = _tile(N, 512)
    bidx = lambda m, n: ((m * tm) // S, 0, 0)
    return pl.pallas_call(
        _inproj_kernel,
        grid=(T // tm, N // tn),
        in_specs=[
            pl.BlockSpec((tm, D), lambda m, n: (m, 0), pipeline_mode=pl.Buffered(1)),
            pl.BlockSpec((1, D), lambda m, n: (0, 0)),
            pl.BlockSpec((1, 1, D), bidx),
            pl.BlockSpec((1, 1, D), bidx),
            pl.BlockSpec((D, tn), lambda m, n: (0, n)),
            pl.BlockSpec((D, NS), lambda m, n: (0, 0)),
            pl.BlockSpec((1, NS), lambda m, n: (0, 0)),
        ],
        out_specs=[
            pl.BlockSpec((tm, tn), lambda m, n: (m, n)),
            pl.BlockSpec((tm, NS), lambda m, n: (m, 0)),
        ],
        out_shape=[
            jax.ShapeDtypeStruct((T, N), BF16),
            jax.ShapeDtypeStruct((T, NS), F32),
        ],
        scratch_shapes=[pltpu.VMEM((tm, D), BF16)],
        compiler_params=_params(("parallel", "arbitrary"), 56),
        name="in_projection",
    )(x2, nw, scale, shift, w_main, w_small, b_small)


def _hy_pre_kernel(u_ref, w_ref, b_ref, o_ref):
    S, tn = u_ref.shape[1], u_ref.shape[2]
    R = min(256, S)
    halo = 16
    nchunks = S // R
    w = w_ref[...]
    b = b_ref[...]
    row = lax.broadcasted_iota(jnp.int32, (R, tn), 0)

    def body(i, carry):
        r0 = pl.multiple_of(i * R, R)
        x = u_ref[0, pl.ds(r0, R), :].astype(F32)
        lo = pl.multiple_of(jnp.maximum(r0 - halo, 0), halo)
        hi = pl.multiple_of(jnp.minimum(r0 + R, S - halo), halo)
        before = u_ref[0, pl.ds(lo, halo), :].astype(F32)[halo - 1:halo]
        after = u_ref[0, pl.ds(hi, halo), :].astype(F32)[0:1]
        before = jnp.where(i == 0, 0.0, before)
        after = jnp.where(i == nchunks - 1, 0.0, after)
        prev = jnp.where(row == 0, before, pltpu.roll(x, 1, 0))
        nxt = jnp.where(row == R - 1, after, pltpu.roll(x, R - 1, 0))
        y = prev * w[0:1] + x * w[1:2] + nxt * w[2:3] + b
        o_ref[0, pl.ds(r0, R), :] = y.astype(BF16)
        return carry

    lax.fori_loop(0, nchunks, body, 0)


def _hy_pre(proj3, conv_w, conv_b, C):
    B, S, _ = proj3.shape
    W = 3 * C
    tn = _tile(W, 512)
    return pl.pallas_call(
        _hy_pre_kernel,
        grid=(B, W // tn),
        in_specs=[
            pl.BlockSpec((1, S, tn), lambda b, j: (b, 0, j)),
            pl.BlockSpec((3, tn), lambda b, j: (0, j)),
            pl.BlockSpec((1, tn), lambda b, j: (0, j)),
        ],
        out_specs=pl.BlockSpec((1, S, tn), lambda b, j: (b, 0, j)),
        out_shape=jax.ShapeDtypeStruct((B, S, W), BF16),
        compiler_params=_params(("parallel", "parallel"), 48),
        name="hyena_depthwise_conv",
    )(proj3, conv_w, conv_b.reshape(1, W))


def _hy_filter_kernel(z_ref, w1_ref, b1_ref, w2_ref, b2_ref, sf_ref, w3f_ref, w3b_ref,
                      df_ref, db_ref, kp_ref, km_ref, kn_ref, ff_scr, fb_scr):
    L, tc = ff_scr.shape
    R = min(256, L)
    sf = sf_ref[...]
    dec_f = jnp.abs(df_ref[...])
    dec_b = jnp.abs(db_ref[...])
    row = lax.broadcasted_iota(jnp.int32, (R, tc), 0)
    sign = jnp.where((row & 1) == 0, 1.0, -1.0)

    def windowed(i, carry):
        ss, kn = carry
        r = pl.ds(pl.multiple_of(i * R, R), R)
        z = z_ref[r, :]
        t_unit = z[:, 0:1]
        h = jnp.sin(sf[0:1] * (_dot_hi(z, w1_ref[...]) + b1_ref[...]))
        h = jnp.sin(sf[1:2] * (_dot_hi(h, w2_ref[...]) + b2_ref[...]))
        ff = _dot_hi(h, w3f_ref[...]) * jnp.exp(-t_unit * dec_f)
        fb = _dot_hi(h, w3b_ref[...]) * jnp.exp(-t_unit * dec_b)
        fb = jnp.where(jnp.logical_and(row == 0, i == 0), 0.0, fb)
        ff_scr[r, :] = ff
        fb_scr[r, :] = fb
        ss = ss + jnp.sum(ff * ff, axis=0, keepdims=True) + jnp.sum(fb * fb, axis=0, keepdims=True)
        kn = kn + jnp.sum((ff + fb) * sign, axis=0, keepdims=True)
        return ss, kn

    zero = jnp.zeros((1, tc), F32)
    ss, kn = lax.fori_loop(0, L // R, windowed, (zero, zero))
    scale = lax.rsqrt(ss + EPS)
    kn_ref[...] = kn * scale

    def normalised(i, carry):
        r = pl.ds(pl.multiple_of(i * R, R), R)
        ff = ff_scr[r, :]
        fb = fb_scr[r, :]
        kp_ref[r, :] = ((ff + fb) * scale).astype(BF16)
        km_ref[r, :] = ((fb - ff) * scale).astype(BF16)
        return carry

    lax.fori_loop(0, L // R, normalised, 0)


def _hy_filters(zpos, w1, b1, w2, b2, sf, w3, decay, C):
    L, E = zpos.shape
    F = w1.shape[1]
    tc = _tile(C, 256)
    nt = C // tc
    fwd = lambda o, j: (0, o * nt + j)
    bwd = lambda o, j: (0, (HY_ORDER + o) * nt + j)
    const = lambda o, j: (0, 0)
    return pl.pallas_call(
        _hy_filter_kernel,
        grid=(HY_ORDER, nt),
        in_specs=[
            pl.BlockSpec((L, E), const),
            pl.BlockSpec((E, F), const),
            pl.BlockSpec((1, F), const),
            pl.BlockSpec((F, F), const),
            pl.BlockSpec((1, F), const),
            pl.BlockSpec((2, F), const),
            pl.BlockSpec((F, tc), fwd),
            pl.BlockSpec((F, tc), bwd),
            pl.BlockSpec((1, tc), fwd),
            pl.BlockSpec((1, tc), bwd),
        ],
        out_specs=[
            pl.BlockSpec((L, tc), fwd),
            pl.BlockSpec((L, tc), fwd),
            pl.BlockSpec((1, tc), fwd),
        ],
        out_shape=[
            jax.ShapeDtypeStruct((L, HY_ORDER * C), BF16),
            jax.ShapeDtypeStruct((L, HY_ORDER * C), BF16),
            jax.ShapeDtypeStruct((1, HY_ORDER * C), F32),
        ],
        scratch_shapes=[pltpu.VMEM((L, tc), F32), pltpu.VMEM((L, tc), F32)],
        compiler_params=_params(("parallel", "parallel"), 40),
        name="hyena_filters",
    )(zpos, w1, b1, w2, b2, sf, w3, w3, decay, decay)


def _first_bin_mask(shape):
    row = lax.broadcasted_iota(jnp.int32, shape, 0)
    return jnp.logical_and(row == 0, pl.program_id(0) == 0)


def _spec_filter_kernel(cm_ref, sm_ref, kp_ref, km_ref, kn_ref, kr_ref, ki_ref):
    kr_ref[...] = _dot(cm_ref[...], kp_ref[...])
    ki = _dot(sm_ref[...], km_ref[...])
    ki_ref[...] = jnp.where(_first_bin_mask(ki.shape), kn_ref[...], ki)


def _hy_filter_spectrum(cm, sm, kp, km, kn):
    L, W = kp.shape
    tk = _tile(L, 512)
    tn = _tile(W, 512)
    return pl.pallas_call(
        _spec_filter_kernel,
        grid=(L // tk, W // tn),
        in_specs=[
            pl.BlockSpec((tk, L), lambda k, j: (k, 0)),
            pl.BlockSpec((tk, L), lambda k, j: (k, 0)),
            pl.BlockSpec((L, tn), lambda k, j: (0, j)),
            pl.BlockSpec((L, tn), lambda k, j: (0, j)),
            pl.BlockSpec((1, tn), lambda k, j: (0, j)),
        ],
        out_specs=[
            pl.BlockSpec((tk, tn), lambda k, j: (k, j)),
            pl.BlockSpec((tk, tn), lambda k, j: (k, j)),
        ],
        out_shape=[jax.ShapeDtypeStruct((L, W), F32)] * 2,
        compiler_params=_params(("parallel", "parallel"), 48),
        name="hyena_filter_spectrum",
    )(cm, sm, kp, km, kn)


def _spec_data_kernel(cm_ref, sm_ref, z_ref, kr_ref, ki_ref, pr_ref, pi_ref):
    z = z_ref[0]
    zr = _dot(cm_ref[...], z)
    zs = _dot(sm_ref[...], z)
    kr = kr_ref[...]
    ki = ki_ref[...]
    m0 = _first_bin_mask(zr.shape)
    zski = zs * ki
    pr_ref[0] = (zr * kr + jnp.where(m0, 0.0, zski)).astype(BF16)
    pi_ref[0] = jnp.where(m0, zski, zr * ki - zs * kr).astype(BF16)


def _hy_spectrum_product(cm, sm, zarr, zcol, kr, ki, order, C):
    B, L, _ = zarr.shape
    tk = _tile(L, 512)
    tn = _tile(C, 512)
    nt = C // tn
    zoff = zcol // tn
    return pl.pallas_call(
        _spec_data_kernel,
        grid=(L // tk, B, nt),
        in_specs=[
            pl.BlockSpec((tk, L), lambda k, b, j: (k, 0)),
            pl.BlockSpec((tk, L), lambda k, b, j: (k, 0)),
            pl.BlockSpec((1, L, tn), lambda k, b, j: (b, 0, zoff + j)),
            pl.BlockSpec((tk, tn), lambda k, b, j: (k, order * nt + j)),
            pl.BlockSpec((tk, tn), lambda k, b, j: (k, order * nt + j)),
        ],
        out_specs=[
            pl.BlockSpec((1, tk, tn), lambda k, b, j: (b, k, j)),
            pl.BlockSpec((1, tk, tn), lambda k, b, j: (b, k, j)),
        ],
        out_shape=[jax.ShapeDtypeStruct((B, L, C), BF16)] * 2,
        compiler_params=_params(("parallel", "parallel", "parallel"), 48),
        name="hyena_spectrum_product",
    )(cm, sm, zarr, kr, ki)


def _inv_kernel(icm_ref, ism_ref, pr_ref, pi_ref, g_ref, z_ref, b_ref, o_ref):
    y = _dot(icm_ref[...], pr_ref[0]) + _dot(ism_ref[...], pi_ref[0])
    z = z_ref[0].astype(F32)
    o_ref[0] = (g_ref[0].astype(F32) * (y + z * b_ref[0])).astype(BF16)


def _hy_inverse_gate(icm, ism, pr, pi, u3, zarr, zcol, bias, order, C):
    B, L, _ = pr.shape
    tt = _tile(L, 512)
    tn = _tile(C, 512)
    nt = C // tn
    zoff = zcol // tn
    return pl.pallas_call(
        _inv_kernel,
        grid=(L // tt, B, nt),
        in_specs=[
            pl.BlockSpec((tt, L), lambda t, b, j: (t, 0)),
            pl.BlockSpec((tt, L), lambda t, b, j: (t, 0)),
            pl.BlockSpec((1, L, tn), lambda t, b, j: (b, 0, j)),
            pl.BlockSpec((1, L, tn), lambda t, b, j: (b, 0, j)),
            pl.BlockSpec((1, tt, tn), lambda t, b, j: (b, t, order * nt + j)),
            pl.BlockSpec((1, tt, tn), lambda t, b, j: (b, t, zoff + j)),
            pl.BlockSpec((1, 1, tn), lambda t, b, j: (order, 0, j)),
        ],
        out_specs=pl.BlockSpec((1, tt, tn), lambda t, b, j: (b, t, j)),
        out_shape=jax.ShapeDtypeStruct((B, L, C), BF16),
        compiler_params=_params(("parallel", "parallel", "parallel"), 48),
        name="hyena_inverse_gate",
    )(icm, ism, pr, pi, u3, zarr, bias)


def _dft_matrices(L):
    N = 2 * L
    k = jnp.arange(L, dtype=jnp.int32)
    ang = ((k[:, None] * k[None, :]) % N).astype(F32) * (2.0 * math.pi / N)
    cm = jnp.cos(ang)
    sm = jnp.sin(ang)
    alt = jnp.where(k % 2 == 0, 1.0, -1.0).astype(F32)
    smf = sm.at[0, :].set(alt)
    icm = (cm * (2.0 / N)).at[:, 0].set(1.0 / N)
    ism = (sm * (-2.0 / N)).at[:, 0].set(alt / N)
    return cm.astype(BF16), smf.astype(BF16), icm.astype(BF16), ism.astype(BF16)


def _hy_positions(L, bands, width):
    t = jnp.arange(L, dtype=F32)
    t_unit = t / (L - 1)
    freqs = jnp.linspace(1e-4, bands - 1, bands, dtype=F32)
    ang = (2.0 * math.pi / L) * t[:, None] * freqs[None, :]
    z = jnp.concatenate([t_unit[:, None], jnp.cos(ang), jnp.sin(ang)], axis=-1)
    return jnp.pad(z, ((0, 0), (0, width - z.shape[1])))


def _log_sigmoid(x):
    return jnp.minimum(x, 0.0) - jnp.log(1.0 + jnp.exp(-jnp.abs(x)))


def _cumsum_lanes(x):
    lane = lax.broadcasted_iota(jnp.int32, x.shape, 1)
    sh = 1
    while sh < x.shape[1]:
        x = x + jnp.where(lane >= sh, pltpu.roll(x, sh, 1), 0.0)
        sh *= 2
    return x


def _mlstm_kernel(q_ref, k_ref, v_ref, o_ref, g_ref, hn_ref, y_ref,
                  qt_scr, vt_scr, ht_scr, bc_scr, st_scr, *, heads, hp):
    NC, Dh = qt_scr.shape[1], qt_scr.shape[2]
    AUG = vt_scr.shape[2]
    Lc = ML_CHUNK
    first = pl.program_id(1) * hp
    kscale = Dh ** -0.5
    ri = lax.broadcasted_iota(jnp.int32, (Lc, Lc), 0)
    ci = lax.broadcasted_iota(jnp.int32, (Lc, Lc), 1)
    valid = (ri <= ci, ri >= ci)
    chains = [(slot, direction) for slot in range(hp) for direction in (0, 1)]
    ones_row = jnp.where(lax.broadcasted_iota(jnp.int32, (AUG - Dh, Lc), 0) == 0, 1.0, 0.0).astype(BF16)

    for idx, (slot, direction) in enumerate(chains):
        logf = _log_sigmoid(g_ref[0, (2 * direction + 1) * heads + first + slot])
        pre = _cumsum_lanes(logf)
        if direction == 0:
            bc_scr[idx] = pre
        else:
            bc_scr[idx] = jnp.sum(logf, axis=1, keepdims=True) - pre + logf
        st_scr[idx] = jnp.zeros((AUG, Dh), F32)

    def transpose_in(n, carry):
        rows = pl.ds(pl.multiple_of(n * Lc, Lc), Lc)
        for slot in range(hp):
            cols = slice(slot * Dh, (slot + 1) * Dh)
            qt_scr[slot, n] = q_ref[0, rows, cols].astype(F32).T.astype(BF16)
            vt_scr[slot, n, 0:Dh, :] = v_ref[0, rows, cols].astype(F32).T.astype(BF16)
            vt_scr[slot, n, Dh:AUG, :] = ones_row
        return carry

    lax.fori_loop(0, NC, transpose_in, 0)

    def step(idx, slot, direction, n, m_st):
        cols = slice(slot * Dh, (slot + 1) * Dh)
        pos = pl.ds(pl.multiple_of(n * Lc, Lc), Lc)
        k = k_ref[0, pos, cols]
        qt = qt_scr[slot, n]
        vta = vt_scr[slot, n]
        state = st_scr[idx]
        bc_r = bc_scr[idx, pl.ds(n, 1), :]
        ig_r = g_ref[0, (2 * direction) * heads + first + slot, pl.ds(n, 1), :]
        g_tot = bc_r[:, Lc - 1:Lc] if direction == 0 else bc_r[:, 0:1]
        gmat = jnp.broadcast_to(ig_r - bc_r, (Lc, Lc)).T
        kq = _dot(k, qt)
        yield
        dmat = jnp.where(valid[direction], bc_r + gmat, -jnp.inf)
        inter = bc_r + m_st
        m_t = jnp.maximum(inter, jnp.max(dmat, axis=0, keepdims=True))
        st = (kq * (jnp.exp(dmat - m_t) * kscale)).astype(BF16)
        qte = (qt.astype(F32) * jnp.exp(inter - m_t)).astype(BF16)
        nd = _dot(jnp.concatenate([vta, state.astype(BF16)], axis=1),
                  jnp.concatenate([st, qte], axis=0))
        w_r = g_tot - bc_r + ig_r
        m_loc = jnp.max(w_r, axis=1, keepdims=True)
        e_w = jnp.exp(w_r - m_loc) * kscale
        loc = _dot((vta.astype(F32) * e_w).astype(BF16), k)
        yield
        den = nd[Dh:Dh + 1, :]
        ht_scr[direction, slot, n] = nd[0:Dh, :] / jnp.maximum(jnp.abs(den), jnp.exp(-m_t))
        m_new = jnp.maximum(g_tot + m_st, m_loc)
        st_scr[idx] = jnp.exp(g_tot + m_st - m_new) * state + jnp.exp(m_loc - m_new) * loc
        return m_new

    def body(i, carry):
        gens = [step(idx, slot, direction, i if direction == 0 else NC - 1 - i, carry[idx])
                for idx, (slot, direction) in enumerate(chains)]
        out = [None] * len(gens)
        while any(o is None for o in out):
            for idx, gen in enumerate(gens):
                if out[idx] is None:
                    try:
                        next(gen)
                    except StopIteration as done:
                        out[idx] = done.value
        return tuple(out)

    lax.fori_loop(0, NC, body, tuple(jnp.zeros((1, 1), F32) for _ in chains))

    hn_t = [jnp.broadcast_to(hn_ref[:, slot * Dh:(slot + 1) * Dh], (Lc, Dh)).T for slot in range(hp)]

    def combine(n, carry):
        pos = pl.ds(pl.multiple_of(n * Lc, Lc), Lc)
        for slot in range(hp):
            cols = slice(slot * Dh, (slot + 1) * Dh)
            hs = ht_scr[0, slot, n] + ht_scr[1, slot, n]
            hs = hs * lax.rsqrt(jnp.mean(hs * hs, axis=0, keepdims=True) + EPS) * hn_t[slot]
            y_ref[0, pos, cols] = (jax.nn.sigmoid(o_ref[0, pos, cols].astype(F32)) * hs.T).astype(BF16)
        return carry

    lax.fori_loop(0, NC, combine, 0)


def _mlstm(proj3, gate_rows, head_norm, C, heads):
    B, S, _ = proj3.shape
    Dh = C // heads
    NC = S // ML_CHUNK
    assert Dh == LANES, "the head dimension must fill one lane tile"
    hp = 2 if heads % 2 == 0 else 1
    aug = Dh + 16
    W = hp * Dh
    base = 3 * C // W
    col = lambda off: (lambda b, h: (b, 0, base + off * (heads // hp) + h))
    return pl.pallas_call(
        functools.partial(_mlstm_kernel, heads=heads, hp=hp),
        grid=(B, heads // hp),
        in_specs=[
            pl.BlockSpec((1, S, W), col(0)),
            pl.BlockSpec((1, S, W), col(1)),
            pl.BlockSpec((1, S, W), col(2)),
            pl.BlockSpec((1, S, W), col(3)),
            pl.BlockSpec((1, 4 * heads, NC, ML_CHUNK), lambda b, h: (b, 0, 0, 0)),
            pl.BlockSpec((1, W), lambda b, h: (0, h)),
        ],
        out_specs=pl.BlockSpec((1, S, W), lambda b, h: (b, 0, h)),
        out_shape=jax.ShapeDtypeStruct((B, S, C), BF16),
        scratch_shapes=[pltpu.VMEM((hp, NC, Dh, ML_CHUNK), BF16),
                        pltpu.VMEM((hp, NC, aug, ML_CHUNK), BF16),
                        pltpu.VMEM((2, hp, NC, Dh, ML_CHUNK), F32),
                        pltpu.VMEM((2 * hp, NC, ML_CHUNK), F32),
                        pltpu.VMEM((2 * hp, aug, Dh), F32)],
        compiler_params=_params(("parallel", "parallel"), 48),
        name="mlstm",
    )(proj3, proj3, proj3, proj3, gate_rows, head_norm)


def _cmul(ar, ai, br, bi):
    return ar * br - ai * bi, ar * bi + ai * br


def _slab_interleave(re, im, axis):
    axis = axis % re.ndim
    n = re.shape[axis]
    shp = re.shape[:axis] + (n // LANES, 1, LANES) + re.shape[axis + 1:]
    out = jnp.concatenate([re.reshape(shp), im.reshape(shp)], axis=axis + 1)
    return out.reshape(re.shape[:axis] + (2 * n,) + re.shape[axis + 1:])


def _s5_kernel(u_ref, w_ref, cm_ref, e_ref, dm_ref, lam_ref, d_ref, y_ref, yf_scr, yb_scr):
    S = yf_scr.shape[0]
    T = e_ref.shape[2]
    P2 = e_ref.shape[3]
    SL = 2 * LANES
    NCH = S // T
    ri = lax.broadcasted_iota(jnp.int32, (T, T), 0)
    ci = lax.broadcasted_iota(jnp.int32, (T, T), 1)
    tris = (jnp.where(ri >= ci, 1.0, 0.0).astype(BF16), jnp.where(ri <= ci, 1.0, 0.0).astype(BF16))
    outs = (yf_scr, yb_scr)

    UN = 2 if NCH % 2 == 0 else 1
    chains = [(d, k, c0) for d in (0, 1) for k in range(UN) for c0 in range(0, P2, SL)]

    def body(i, carry):
        rows = {}
        for k in range(UN):
            rows[0, k] = pl.ds(pl.multiple_of((i * UN + k) * T, T), T)
            rows[1, k] = pl.ds(pl.multiple_of((NCH - 1 - i * UN - k) * T, T), T)
        u = {key: u_ref[0, r, :] for key, r in rows.items()}
        x = [_dot(u[d, k], w_ref[d, 0, :, c0:c0 + SL]) for d, k, c0 in chains]
        xs = []
        for (d, k, c0), xv in zip(chains, x):
            sr, si = _cmul(e_ref[d, 0, :, c0:c0 + LANES], e_ref[d, 0, :, c0 + LANES:c0 + SL],
                           xv[:, :LANES], xv[:, LANES:])
            xs.append(jnp.concatenate([sr, si], axis=1).astype(BF16))
        cs = [_dot(tris[d], v) for (d, k, c0), v in zip(chains, xs)]
        hs = {}
        state = {(d, c0): carry[d][:, c0:c0 + SL] for d in (0, 1) for c0 in range(0, P2, SL)}
        for (d, k, c0), cv in zip(chains, cs):
            re = slice(c0, c0 + LANES)
            im = slice(c0 + LANES, c0 + SL)
            st = state[d, c0]
            br, bi = _cmul(lam_ref[d, 0, :, re], lam_ref[d, 0, :, im],
                           st[:, :LANES], st[:, LANES:])
            hr, hi = _cmul(dm_ref[d, 0, :, re], dm_ref[d, 0, :, im],
                           cv[:, :LANES] + br, cv[:, LANES:] + bi)
            h = jnp.concatenate([hr, hi], axis=1)
            last = T - 1 if d == 0 else 0
            state[d, c0] = h[last:last + 1, :]
            hs[d, k, c0] = h
        ys = {key: _dot(h.astype(BF16), cm_ref[0, key[2]:key[2] + SL, :]) for key, h in hs.items()}
        for (d, k), r in rows.items():
            outs[d][r, :] = sum(ys[d, k, c0] for c0 in range(0, P2, SL))
        return tuple(jnp.concatenate([state[d, c0] for c0 in range(0, P2, SL)], axis=1) for d in (0, 1))

    zero = jnp.zeros((1, P2), F32)
    lax.fori_loop(0, NCH // UN, body, (zero, zero))

    R = min(512, S)
    d = d_ref[...]

    def skip_gelu(i, carry):
        r = pl.ds(pl.multiple_of(i * R, R), R)
        y = yf_scr[r, :] + yb_scr[r, :] + d * u_ref[0, r, :].astype(F32)
        y_ref[0, r, :] = jax.nn.gelu(y).astype(BF16)
        return carry

    lax.fori_loop(0, S // R, skip_gelu, 0)


def _s5(proj3, w_bd, c_bd, e_tab, d_tab, lam_tab, d_skip, C):
    B, S, _ = proj3.shape
    NB, P2, U = c_bd.shape
    T = e_tab.shape[2]
    base = 7 * C // U
    return pl.pallas_call(
        _s5_kernel,
        grid=(B, NB),
        in_specs=[
            pl.BlockSpec((1, S, U), lambda b, j: (b, 0, base + j)),
            pl.BlockSpec((2, 1, U, P2), lambda b, j: (0, j, 0, 0)),
            pl.BlockSpec((1, P2, U), lambda b, j: (j, 0, 0)),
            pl.BlockSpec((2, 1, T, P2), lambda b, j: (0, j, 0, 0)),
            pl.BlockSpec((2, 1, T, P2), lambda b, j: (0, j, 0, 0)),
            pl.BlockSpec((2, 1, 1, P2), lambda b, j: (0, j, 0, 0)),
            pl.BlockSpec((1, U), lambda b, j: (0, j)),
        ],
        out_specs=pl.BlockSpec((1, S, U), lambda b, j: (b, 0, j)),
        out_shape=jax.ShapeDtypeStruct((B, S, C), BF16),
        scratch_shapes=[pltpu.VMEM((S, U), F32), pltpu.VMEM((S, U), F32)],
        compiler_params=_params(("parallel", "parallel"), 32),
        name="s5_scan",
    )(proj3, w_bd, c_bd, e_tab, d_tab, lam_tab, d_skip)


def _s5_tables(lam_re, lam_im, log_step, b_re, b_im, c_re, c_im):
    _, G, P = lam_re.shape
    Hg = b_re.shape[-1]
    gb = S5_BLOCK_GROUPS
    NB = G // gb
    T = S5_CHUNK
    lre = jnp.minimum(lam_re, -1e-4)
    lim = lam_im
    step = jnp.exp(log_step)[..., None]
    mag = jnp.exp(lre * step)
    ang = lim * step
    lbr = mag * jnp.cos(ang)
    lbi = mag * jnp.sin(ang)
    den = lre * lre + lim * lim
    qr = ((lbr - 1.0) * lre + lbi * lim) / den
    qi = (lbi * lre - (lbr - 1.0) * lim) / den
    bbr = qr[..., None] * b_re[None] - qi[..., None] * b_im[None]
    bbi = qr[..., None] * b_im[None] + qi[..., None] * b_re[None]
    eye = jnp.eye(gb, dtype=F32)

    def drive(bb):
        t = bb.reshape(2, NB, gb, P, Hg).transpose(0, 1, 2, 4, 3)
        return jnp.einsum("ab,djahp->djahbp", eye, t).reshape(2, NB, gb * Hg, gb * P)

    w_bd = _slab_interleave(drive(bbr), drive(bbi), -1).astype(BF16)

    def readout(cc):
        t = cc.reshape(NB, gb, Hg, P).transpose(0, 1, 3, 2)
        return jnp.einsum("ab,japh->japbh", eye, t).reshape(NB, gb * P, gb * Hg)

    c_bd = _slab_interleave(readout(c_re), -readout(c_im), 1).astype(BF16)

    def blocks(a):
        return a.reshape(2, -1, NB, gb * P).transpose(0, 2, 1, 3)

    r = jnp.arange(T, dtype=F32)
    r = jnp.stack([r, T - 1.0 - r])[:, :, None, None]
    lm = (lre * step)[:, None]
    an = ang[:, None]
    e_tab = _slab_interleave(blocks(jnp.exp(-r * lm) * jnp.cos(-r * an)),
                             blocks(jnp.exp(-r * lm) * jnp.sin(-r * an)), -1)
    d_tab = _slab_interleave(blocks(jnp.exp(r * lm) * jnp.cos(r * an)),
                             blocks(jnp.exp(r * lm) * jnp.sin(r * an)), -1)
    lam_tab = _slab_interleave(blocks(lbr[:, None]), blocks(lbi[:, None]), -1)
    return w_bd, c_bd, e_tab, d_tab, lam_tab


def _merge_kernel(ya_ref, yb_ref, yc_ref, gl_ref, gw_ref, gb_ref, wg_ref, bg_ref, wb_ref,
                  o_ref, yc_scr):
    @pl.when(pl.program_id(1) == 0)
    def _():
        y = yc_ref[...]
        gate = jax.nn.sigmoid(_dot(y, gw_ref[...]) + gb_ref[...])
        yc_scr[...] = (y.astype(F32) * gate).astype(BF16)

    gl = gl_ref[...].astype(BF16)
    acc = None
    for n, br in enumerate((ya_ref[...], yb_ref[...], yc_scr[...])):
        term = jax.nn.sigmoid(_dot(gl, wg_ref[n]) + bg_ref[n]) * _dot(br, wb_ref[n])
        acc = term if acc is None else acc + term
    o_ref[...] = acc.astype(BF16)


def _merge(ya, yb, yc, small, glu_w, glu_b, wg, bg, wb):
    T, C = ya.shape
    NBR, R, D = wg.shape
    tm = _tile(T, TOKEN_TILE)
    tn = _tile(D, 512)
    row = lambda m, n: (m, 0)
    return pl.pallas_call(
        _merge_kernel,
        grid=(T // tm, D // tn),
        in_specs=[
            pl.BlockSpec((tm, C), row),
            pl.BlockSpec((tm, C), row),
            pl.BlockSpec((tm, C), row),
            pl.BlockSpec((tm, R), row),
            pl.BlockSpec((C, C), lambda m, n: (0, 0)),
            pl.BlockSpec((1, C), lambda m, n: (0, 0)),
            pl.BlockSpec((NBR, R, tn), lambda m, n: (0, 0, n)),
            pl.BlockSpec((NBR, 1, tn), lambda m, n: (0, 0, n)),
            pl.BlockSpec((NBR, C, tn), lambda m, n: (0, 0, n)),
        ],
        out_specs=pl.BlockSpec((tm, tn), lambda m, n: (m, n)),
        out_shape=jax.ShapeDtypeStruct((T, D), BF16),
        scratch_shapes=[pltpu.VMEM((tm, C), BF16)],
        compiler_params=_params(("parallel", "arbitrary"), 48),
        name="branch_merge",
    )(ya, yb, yc, small, glu_w, glu_b, wg, bg, wb)


def _proj_res_kernel(a_ref, w_ref, x_ref, g_ref, o_ref):
    o_ref[...] = x_ref[...] + g_ref[0] * _dot(a_ref[...], w_ref[...])


def _proj_residual(a, w, x2, gate, S, tn_target, a_buffers, name):
    T, K = a.shape
    D = w.shape[1]
    tm = _tile(S, TOKEN_TILE)
    tn = _tile(D, tn_target)
    return pl.pallas_call(
        _proj_res_kernel,
        grid=(T // tm, D // tn),
        in_specs=[
            pl.BlockSpec((tm, K), lambda m, n: (m, 0), pipeline_mode=pl.Buffered(a_buffers)),
            pl.BlockSpec((K, tn), lambda m, n: (0, n)),
            pl.BlockSpec((tm, tn), lambda m, n: (m, n)),
            pl.BlockSpec((1, 1, tn), lambda m, n: ((m * tm) // S, 0, n)),
        ],
        out_specs=pl.BlockSpec((tm, tn), lambda m, n: (m, n)),
        out_shape=jax.ShapeDtypeStruct((T, D), F32),
        compiler_params=_params(("parallel", "parallel"), 52),
        name=name,
    )(a, w, x2, gate)


def _ffn_up_kernel(x_ref, nw_ref, sc_ref, sh_ref, wg_ref, wu_ref, o_ref, h_scr):
    @pl.when(pl.program_id(1) == 0)
    def _():
        _norm_mod_rows(x_ref, nw_ref, sc_ref, sh_ref, h_scr)

    h = h_scr[...]
    g = _dot(h, wg_ref[...])
    o_ref[...] = (g * jax.nn.sigmoid(g) * _dot(h, wu_ref[...])).astype(BF16)


def _ffn_up(x2, S, nw, scale, shift, wg, wu):
    T, D = x2.shape
    N = wg.shape[1]
    tm = _tile(S, TOKEN_TILE)
    tn = _tile(N, 512)
    bidx = lambda m, n: ((m * tm) // S, 0, 0)
    return pl.pallas_call(
        _ffn_up_kernel,
        grid=(T // tm, N // tn),
        in_specs=[
            pl.BlockSpec((tm, D), lambda m, n: (m, 0), pipeline_mode=pl.Buffered(1)),
            pl.BlockSpec((1, D), lambda m, n: (0, 0)),
            pl.BlockSpec((1, 1, D), bidx),
            pl.BlockSpec((1, 1, D), bidx),
            pl.BlockSpec((D, tn), lambda m, n: (0, n)),
            pl.BlockSpec((D, tn), lambda m, n: (0, n)),
        ],
        out_specs=pl.BlockSpec((tm, tn), lambda m, n: (m, n)),
        out_shape=jax.ShapeDtypeStruct((T, N), BF16),
        scratch_shapes=[pltpu.VMEM((tm, D), BF16)],
        compiler_params=_params(("parallel", "arbitrary"), 52),
        name="swiglu_up",
    )(x2, nw, scale, shift, wg, wu)


def _final_norm_kernel(x_ref, w_ref, o_ref):
    rows = min(NORM_ROWS, x_ref.shape[0])
    w = w_ref[...]

    def body(i, carry):
        r = pl.ds(pl.multiple_of(i * rows, rows), rows)
        x = x_ref[r, :]
        o_ref[r, :] = x * lax.rsqrt(jnp.mean(x * x, axis=-1, keepdims=True) + EPS) * w
        return carry

    lax.fori_loop(0, x_ref.shape[0] // rows, body, 0)


def _final_norm(x2, w):
    T, D = x2.shape
    tm = _tile(T, 256)
    return pl.pallas_call(
        _final_norm_kernel,
        grid=(T // tm,),
        in_specs=[pl.BlockSpec((tm, D), lambda m: (m, 0)), pl.BlockSpec((1, D), lambda m: (0, 0))],
        out_specs=pl.BlockSpec((tm, D), lambda m: (m, 0)),
        out_shape=jax.ShapeDtypeStruct((T, D), F32),
        compiler_params=_params(("parallel",), 40),
        name="final_norm",
    )(x2, w)


def _pad_to(a, axis, size):
    pad = [(0, 0)] * a.ndim
    pad[axis] = (0, size - a.shape[axis])
    return jnp.pad(a, pad)


def kernel(x, c, w_cond, w_mod, b_mod, norm_mix, norm_ffn, w_in, b_mgate, hy_conv_w, hy_conv_b, hy_f_w1, hy_f_b1, hy_f_w2, hy_f_b2, hy_f_w3, hy_sin_freq, hy_decay, hy_bias, ml_norm, s5_lam_re, s5_lam_im, s5_log_step, s5_b_re, s5_b_im, s5_c_re, s5_c_im, s5_d, s5_glu_w, s5_glu_b, w_gate_up, b_gate, w_branch, w_out, w_ffn_gate, w_ffn_up, w_ffn_down, norm_final):
    B, S, D = x.shape
    depth = w_in.shape[0]
    C = D // 4
    T = B * S
    n_gate = b_mgate.shape[-1]
    heads = n_gate // 4
    R = w_gate_up.shape[2]
    NS = -(-(R + n_gate) // LANES) * LANES
    NC = S // ML_CHUNK
    bands = (hy_f_w1.shape[1] - 1) // 2
    F = hy_f_w1.shape[2]
    Fp = -(-F // LANES) * LANES

    mod = _modulation(c, w_cond, w_mod, b_mod).reshape(depth, B, 6, 1, D)
    cm, sm, icm, ism = _dft_matrices(S)
    zpos = _hy_positions(S, bands, LANES)

    x2 = x.reshape(T, D)
    for l in range(depth):
        shift1, scale1, gate1, shift2, scale2, gate2 = (mod[l, :, i] for i in range(6))

        wl = w_in[l]
        w_main = jnp.concatenate([wl[:, :7 * C], wl[:, 7 * C + n_gate:8 * C + n_gate]], axis=1)
        w_small = jnp.concatenate([wl[:, 8 * C + n_gate:], wl[:, 7 * C:7 * C + n_gate]], axis=1)
        w_small = _pad_to(w_small, 1, NS)
        b_small = _pad_to(jnp.concatenate([jnp.zeros((R,), F32), b_mgate[l]]), 0, NS).reshape(1, NS)
        proj, small = _inproj(x2, S, norm_mix[l].reshape(1, D), scale1, shift1,
                              w_main.astype(BF16), w_small.astype(BF16), b_small)
        proj3 = proj.reshape(B, S, 8 * C)

        u3 = _hy_pre(proj3, hy_conv_w[l], hy_conv_b[l], C)
        kp, km, kn = _hy_filters(
            zpos, _pad_to(_pad_to(hy_f_w1[l], 0, LANES), 1, Fp), _pad_to(hy_f_b1[l], 0, Fp).reshape(1, Fp),
            _pad_to(_pad_to(hy_f_w2[l], 0, Fp), 1, Fp), _pad_to(hy_f_b2[l], 0, Fp).reshape(1, Fp),
            _pad_to(hy_sin_freq[l], 1, Fp), _pad_to(hy_f_w3[l], 0, Fp),
            hy_decay[l].reshape(1, -1), C)
        kr, ki = _hy_filter_spectrum(cm, sm, kp, km, kn)
        hy_b = hy_bias[l].reshape(HY_ORDER, 1, C)
        zarr, zcol = u3, 2 * C
        for o in range(HY_ORDER):
            pr, pi = _hy_spectrum_product(cm, sm, zarr, zcol, kr, ki, o, C)
            zarr = _hy_inverse_gate(icm, ism, pr, pi, u3, zarr, zcol, hy_b, o, C)
            zcol = 0
        y_a = zarr.reshape(T, C)

        gate_rows = small[:, R:R + n_gate].reshape(B, NC, ML_CHUNK, n_gate).transpose(0, 3, 1, 2)
        y_b = _mlstm(proj3, gate_rows, ml_norm[l].reshape(1, C), C, heads).reshape(T, C)

        w_bd, c_bd, e_tab, d_tab, lam_tab = _s5_tables(
            s5_lam_re[l], s5_lam_im[l], s5_log_step[l], s5_b_re[l], s5_b_im[l], s5_c_re[l], s5_c_im[l])
        y_c = _s5(proj3, w_bd, c_bd, e_tab, d_tab, lam_tab, s5_d[l].reshape(1, C), C).reshape(T, C)

        merged = _merge(y_a, y_b, y_c, small, s5_glu_w[l].astype(BF16), s5_glu_b[l].reshape(1, C),
                        w_gate_up[l].astype(BF16), b_gate[l].reshape(-1, 1, D), w_branch[l].astype(BF16))
        x2 = _proj_residual(merged, w_out[l].astype(BF16), x2, gate1, S, 512, 2, "out_projection")

        hidden = _ffn_up(x2, S, norm_ffn[l].reshape(1, D), scale2, shift2,
                         w_ffn_gate[l].astype(BF16), w_ffn_up[l].astype(BF16))
        x2 = _proj_residual(hidden, w_ffn_down[l].astype(BF16), x2, gate2, S, 256, 1, "swiglu_down")

    return _final_norm(x2, norm_final.reshape(1, D)).reshape(B, S, D)
```

```python
import functools
import math

import jax
import jax.numpy as jnp
from jax import lax
from jax.experimental import pallas as pl
from jax.experimental.pallas import tpu as pltpu

F32 = jnp.float32
BF16 = jnp.bfloat16
EPS = 1e-6
HIGHEST = lax.Precision.HIGHEST

LANES = 128
HY_ORDER = 2
ML_CHUNK = 128
S5_CHUNK = 64
S5_BLOCK_GROUPS = 8
S5_UNROLL = 2
NORM_ROWS = 32
TOKEN_TILE = 1024
MIB = 1024 * 1024


def _tile(dim, target, align=LANES):
    if dim <= target:
        return dim
    t = (target // align) * align
    while t >= align:
        if dim % t == 0:
            return t
        t -= align
    return dim


def _params(sem, vmem_mib):
    return pltpu.CompilerParams(dimension_semantics=sem, vmem_limit_bytes=vmem_mib * MIB)


def _dot(a, b):
    return jnp.dot(a, b, preferred_element_type=F32)


def _dot_hi(a, b):
    return jnp.dot(a, b, preferred_element_type=F32, precision=HIGHEST)


def _mod_kernel(c_ref, wc_ref, wm_ref, bm_ref, o_ref):
    c = c_ref[...]
    cond_h = _dot_hi(c * jax.nn.sigmoid(c), wc_ref[...])
    o_ref[0] = _dot_hi(cond_h, wm_ref[0]) + bm_ref[0]


def _modulation(c, w_cond, w_mod, b_mod):
    B, D = c.shape
    depth, R, W = w_mod.shape
    rows = max(8, B)
    cp = jnp.zeros((rows, D), F32).at[:B].set(c)
    tn = _tile(W, 4096)
    out = pl.pallas_call(
        _mod_kernel,
        grid=(depth, W // tn),
        in_specs=[
            pl.BlockSpec((rows, D), lambda l, n: (0, 0)),
            pl.BlockSpec((D, R), lambda l, n: (0, 0)),
            pl.BlockSpec((1, R, tn), lambda l, n: (l, 0, n)),
            pl.BlockSpec((1, 1, tn), lambda l, n: (l, 0, n)),
        ],
        out_specs=pl.BlockSpec((1, rows, tn), lambda l, n: (l, 0, n)),
        out_shape=jax.ShapeDtypeStruct((depth, rows, W), F32),
        compiler_params=_params(("arbitrary", "arbitrary"), 40),
        name="adaln_modulation",
    )(cp, w_cond, w_mod, b_mod.reshape(depth, 1, W))
    return out[:, :B]


def _norm_mod(x, nw, scale, shift):
    ms = jnp.mean(x * x, axis=-1, keepdims=True)
    y = x * lax.rsqrt(ms + EPS) * nw
    return y * (1.0 + scale) + shift


def _norm_mod_rows(x_ref, nw_ref, sc_ref, sh_ref, h_scr):
    rows = min(NORM_ROWS, x_ref.shape[0])
    nw = nw_ref[...]
    scale = sc_ref[0]
    shift = sh_ref[0]

    def body(i, carry):
        r = pl.ds(pl.multiple_of(i * rows, rows), rows)
        h_scr[r, :] = _norm_mod(x_ref[r, :], nw, scale, shift).astype(BF16)
        return carry

    lax.fori_loop(0, x_ref.shape[0] // rows, body, 0)


def _inproj_kernel(x_ref, nw_ref, sc_ref, sh_ref, w_ref, ws_ref, bs_ref, o_ref, os_ref, h_scr):
    @pl.when(pl.program_id(1) == 0)
    def _():
        _norm_mod_rows(x_ref, nw_ref, sc_ref, sh_ref, h_scr)
        os_ref[...] = _dot(h_scr[...], ws_ref[...]) + bs_ref[...]

    o_ref[...] = _dot(h_scr[...], w_ref[...]).astype(BF16)


def _inproj(x2, S, nw, scale, shift, w_main, w_small, b_small):
    T, D = x2.shape
    N = w_main.shape[1]
    NS = w_small.shape[1]
    tm = _tile(S, TOKEN_TILE)
    tn = _tile(N, 512)
    bidx = lambda m, n: ((m * tm) // S, 0, 0)
    return pl.pallas_call(
        _inproj_kernel,
        grid=(T // tm, N // tn),
        in_specs=[
            pl.BlockSpec((tm, D), lambda m, n: (m, 0), pipeline_mode=pl.Buffered(1)),
            pl.BlockSpec((1, D), lambda m, n: (0, 0)),
            pl.BlockSpec((1, 1, D), bidx),
            pl.BlockSpec((1, 1, D), bidx),
            pl.BlockSpec((D, tn), lambda m, n: (0, n)),
            pl.BlockSpec((D, NS), lambda m, n: (0, 0)),
            pl.BlockSpec((1, NS), lambda m, n: (0, 0)),
        ],
        out_specs=[
            pl.BlockSpec((tm, tn), lambda m, n: (m, n)),
            pl.BlockSpec((tm, NS), lambda m, n: (m, 0)),
        ],
        out_shape=[
            jax.ShapeDtypeStruct((T, N), BF16),
            jax.ShapeDtypeStruct((T, NS), F32),
        ],
        scratch_shapes=[pltpu.VMEM((tm, D), BF16)],
        compiler_params=_params(("parallel", "arbitrary"), 56),
        name="in_projection",
    )(x2, nw, scale, shift, w_main, w_small, b_small)


def _hy_pre_kernel(u_ref, w_ref, b_ref, o_ref):
    S, tn = u_ref.shape[1], u_ref.shape[2]
    R = min(256, S)
    halo = 16
    nchunks = S // R
    w = w_ref[...]
    b = b_ref[...]
    row = lax.broadcasted_iota(jnp.int32, (R, tn), 0)

    def body(i, carry):
        r0 = pl.multiple_of(i * R, R)
        x = u_ref[0, pl.ds(r0, R), :].astype(F32)
        lo = pl.multiple_of(jnp.maximum(r0 - halo, 0), halo)
        hi = pl.multiple_of(jnp.minimum(r0 + R, S - halo), halo)
        before = u_ref[0, pl.ds(lo, halo), :].astype(F32)[halo - 1:halo]
        after = u_ref[0, pl.ds(hi, halo), :].astype(F32)[0:1]
        before = jnp.where(i == 0, 0.0, before)
        after = jnp.where(i == nchunks - 1, 0.0, after)
        prev = jnp.where(row == 0, before, pltpu.roll(x, 1, 0))
        nxt = jnp.where(row == R - 1, after, pltpu.roll(x, R - 1, 0))
        y = prev * w[0:1] + x * w[1:2] + nxt * w[2:3] + b
        o_ref[0, pl.ds(r0, R), :] = y.astype(BF16)
        return carry

    lax.fori_loop(0, nchunks, body, 0)


def _hy_pre(proj3, conv_w, conv_b, C):
    B, S, _ = proj3.shape
    W = 3 * C
    tn = _tile(W, 512)
    return pl.pallas_call(
        _hy_pre_kernel,
        grid=(B, W // tn),
        in_specs=[
            pl.BlockSpec((1, S, tn), lambda b, j: (b, 0, j)),
            pl.BlockSpec((3, tn), lambda b, j: (0, j)),
            pl.BlockSpec((1, tn), lambda b, j: (0, j)),
        ],
        out_specs=pl.BlockSpec((1, S, tn), lambda b, j: (b, 0, j)),
        out_shape=jax.ShapeDtypeStruct((B, S, W), BF16),
        compiler_params=_params(("parallel", "parallel"), 48),
        name="hyena_depthwise_conv",
    )(proj3, conv_w, conv_b.reshape(1, W))


def _hy_filter_kernel(z_ref, w1_ref, b1_ref, w2_ref, b2_ref, sf_ref, w3f_ref, w3b_ref,
                      df_ref, db_ref, kp_ref, km_ref, kn_ref, ff_scr, fb_scr, h_scr):
    L, tc = ff_scr.shape
    R = min(256, L)
    sf = sf_ref[...]
    dec_f = jnp.abs(df_ref[...])
    dec_b = jnp.abs(db_ref[...])
    row = lax.broadcasted_iota(jnp.int32, (R, tc), 0)
    sign = jnp.where((row & 1) == 0, 1.0, -1.0)

    @pl.when(jnp.logical_and(pl.program_id(0) == 0, pl.program_id(1) == 0))
    def _():
        def hidden(i, carry):
            r = pl.ds(pl.multiple_of(i * R, R), R)
            h = jnp.sin(sf[0:1] * (_dot_hi(z_ref[r, :], w1_ref[...]) + b1_ref[...]))
            h_scr[r, :] = jnp.sin(sf[1:2] * (_dot_hi(h, w2_ref[...]) + b2_ref[...]))
            return carry

        lax.fori_loop(0, L // R, hidden, 0)

    def windowed(i, carry):
        ss, kn = carry
        r = pl.ds(pl.multiple_of(i * R, R), R)
        t_unit = z_ref[r, 0:1]
        h = h_scr[r, :]
        ff = _dot_hi(h, w3f_ref[...]) * jnp.exp(-t_unit * dec_f)
        fb = _dot_hi(h, w3b_ref[...]) * jnp.exp(-t_unit * dec_b)
        fb = jnp.where(jnp.logical_and(row == 0, i == 0), 0.0, fb)
        ff_scr[r, :] = ff
        fb_scr[r, :] = fb
        ss = ss + jnp.sum(ff * ff, axis=0, keepdims=True) + jnp.sum(fb * fb, axis=0, keepdims=True)
        kn = kn + jnp.sum((ff + fb) * sign, axis=0, keepdims=True)
        return ss, kn

    zero = jnp.zeros((1, tc), F32)
    ss, kn = lax.fori_loop(0, L // R, windowed, (zero, zero))
    scale = lax.rsqrt(ss + EPS)
    kn_ref[...] = kn * scale

    def normalised(i, carry):
        r = pl.ds(pl.multiple_of(i * R, R), R)
        ff = ff_scr[r, :]
        fb = fb_scr[r, :]
        kp_ref[r, :] = ((ff + fb) * scale).astype(BF16)
        km_ref[r, :] = ((fb - ff) * scale).astype(BF16)
        return carry

    lax.fori_loop(0, L // R, normalised, 0)


def _hy_filters(zpos, w1, b1, w2, b2, sf, w3, decay, C):
    L, E = zpos.shape
    F = w1.shape[1]
    tc = _tile(C, 256)
    nt = C // tc
    fwd = lambda o, j: (0, o * nt + j)
    bwd = lambda o, j: (0, (HY_ORDER + o) * nt + j)
    const = lambda o, j: (0, 0)
    return pl.pallas_call(
        _hy_filter_kernel,
        grid=(HY_ORDER, nt),
        in_specs=[
            pl.BlockSpec((L, E), const),
            pl.BlockSpec((E, F), const),
            pl.BlockSpec((1, F), const),
            pl.BlockSpec((F, F), const),
            pl.BlockSpec((1, F), const),
            pl.BlockSpec((2, F), const),
            pl.BlockSpec((F, tc), fwd),
            pl.BlockSpec((F, tc), bwd),
            pl.BlockSpec((1, tc), fwd),
            pl.BlockSpec((1, tc), bwd),
        ],
        out_specs=[
            pl.BlockSpec((L, tc), fwd),
            pl.BlockSpec((L, tc), fwd),
            pl.BlockSpec((1, tc), fwd),
        ],
        out_shape=[
            jax.ShapeDtypeStruct((L, HY_ORDER * C), BF16),
            jax.ShapeDtypeStruct((L, HY_ORDER * C), BF16),
            jax.ShapeDtypeStruct((1, HY_ORDER * C), F32),
        ],
        scratch_shapes=[pltpu.VMEM((L, tc), F32), pltpu.VMEM((L, tc), F32), pltpu.VMEM((L, F), F32)],
        compiler_params=_params(("arbitrary", "arbitrary"), 40),
        name="hyena_filters",
    )(zpos, w1, b1, w2, b2, sf, w3, w3, decay, decay)


def _first_bin_mask(shape):
    row = lax.broadcasted_iota(jnp.int32, shape, 0)
    return jnp.logical_and(row == 0, pl.program_id(0) == 0)


def _spec_filter_kernel(cm_ref, sm_ref, kp_ref, km_ref, kn_ref, kr_ref, ki_ref):
    kr_ref[...] = _dot(cm_ref[...], kp_ref[...])
    ki = _dot(sm_ref[...], km_ref[...])
    ki_ref[...] = jnp.where(_first_bin_mask(ki.shape), kn_ref[...], ki)


def _hy_filter_spectrum(cm, sm, kp, km, kn):
    L, W = kp.shape
    tk = _tile(L, 512)
    tn = _tile(W, 512)
    return pl.pallas_call(
        _spec_filter_kernel,
        grid=(L // tk, W // tn),
        in_specs=[
            pl.BlockSpec((tk, L), lambda k, j: (k, 0)),
            pl.BlockSpec((tk, L), lambda k, j: (k, 0)),
            pl.BlockSpec((L, tn), lambda k, j: (0, j)),
            pl.BlockSpec((L, tn), lambda k, j: (0, j)),
            pl.BlockSpec((1, tn), lambda k, j: (0, j)),
        ],
        out_specs=[
            pl.BlockSpec((tk, tn), lambda k, j: (k, j)),
            pl.BlockSpec((tk, tn), lambda k, j: (k, j)),
        ],
        out_shape=[jax.ShapeDtypeStruct((L, W), F32)] * 2,
        compiler_params=_params(("parallel", "parallel"), 48),
        name="hyena_filter_spectrum",
    )(cm, sm, kp, km, kn)


def _spec_data_kernel(cm_ref, sm_ref, z_ref, kr_ref, ki_ref, pr_ref, pi_ref):
    z = z_ref[0]
    zr = _dot(cm_ref[...], z)
    zs = _dot(sm_ref[...], z)
    kr = kr_ref[...]
    ki = ki_ref[...]
    m0 = _first_bin_mask(zr.shape)
    zski = zs * ki
    pr_ref[0] = (zr * kr + jnp.where(m0, 0.0, zski)).astype(BF16)
    pi_ref[0] = jnp.where(m0, zski, zr * ki - zs * kr).astype(BF16)


def _hy_spectrum_product(cm, sm, zarr, zcol, kr, ki, order, C):
    B, L, _ = zarr.shape
    tk = _tile(L, 512)
    tn = _tile(C, 512)
    nt = C // tn
    zoff = zcol // tn
    return pl.pallas_call(
        _spec_data_kernel,
        grid=(L // tk, B, nt),
        in_specs=[
            pl.BlockSpec((tk, L), lambda k, b, j: (k, 0)),
            pl.BlockSpec((tk, L), lambda k, b, j: (k, 0)),
            pl.BlockSpec((1, L, tn), lambda k, b, j: (b, 0, zoff + j)),
            pl.BlockSpec((tk, tn), lambda k, b, j: (k, order * nt + j)),
            pl.BlockSpec((tk, tn), lambda k, b, j: (k, order * nt + j)),
        ],
        out_specs=[
            pl.BlockSpec((1, tk, tn), lambda k, b, j: (b, k, j)),
            pl.BlockSpec((1, tk, tn), lambda k, b, j: (b, k, j)),
        ],
        out_shape=[jax.ShapeDtypeStruct((B, L, C), BF16)] * 2,
        compiler_params=_params(("parallel", "parallel", "parallel"), 48),
        name="hyena_spectrum_product",
    )(cm, sm, zarr, kr, ki)


def _inv_kernel(icm_ref, ism_ref, pr_ref, pi_ref, g_ref, z_ref, b_ref, o_ref):
    y = _dot(icm_ref[...], pr_ref[0]) + _dot(ism_ref[...], pi_ref[0])
    z = z_ref[0].astype(F32)
    o_ref[0] = (g_ref[0].astype(F32) * (y + z * b_ref[0])).astype(BF16)


def _hy_inverse_gate(icm, ism, pr, pi, u3, zarr, zcol, bias, order, C):
    B, L, _ = pr.shape
    tt = _tile(L, 512)
    tn = _tile(C, 512)
    nt = C // tn
    zoff = zcol // tn
    return pl.pallas_call(
        _inv_kernel,
        grid=(L // tt, B, nt),
        in_specs=[
            pl.BlockSpec((tt, L), lambda t, b, j: (t, 0)),
            pl.BlockSpec((tt, L), lambda t, b, j: (t, 0)),
            pl.BlockSpec((1, L, tn), lambda t, b, j: (b, 0, j)),
            pl.BlockSpec((1, L, tn), lambda t, b, j: (b, 0, j)),
            pl.BlockSpec((1, tt, tn), lambda t, b, j: (b, t, order * nt + j)),
            pl.BlockSpec((1, tt, tn), lambda t, b, j: (b, t, zoff + j)),
            pl.BlockSpec((1, 1, tn), lambda t, b, j: (order, 0, j)),
        ],
        out_specs=pl.BlockSpec((1, tt, tn), lambda t, b, j: (b, t, j)),
        out_shape=jax.ShapeDtypeStruct((B, L, C), BF16),
        compiler_params=_params(("parallel", "parallel", "parallel"), 48),
        name="hyena_inverse_gate",
    )(icm, ism, pr, pi, u3, zarr, bias)


def _dft_matrices(L):
    N = 2 * L
    tk = _tile(L, 256, 8)
    nb = L // tk
    t = jnp.arange(L, dtype=jnp.int32)

    def phase(rows):
        return ((rows[:, None] * t[None, :]) % N).astype(F32) * (2.0 * math.pi / N)

    pa = phase(jnp.arange(tk, dtype=jnp.int32))
    pb = phase(jnp.arange(nb, dtype=jnp.int32) * tk).reshape(nb, 1, L)
    table = pl.BlockSpec((tk, L), lambda k: (0, 0))
    base = pl.BlockSpec((1, 1, L), lambda k: (k, 0, 0))
    out = pl.BlockSpec((tk, L), lambda k: (k, 0))
    return pl.pallas_call(
        functools.partial(_dft_gen_kernel, n=N),
        grid=(nb,),
        in_specs=[table, table, base, base],
        out_specs=[out] * 4,
        out_shape=[jax.ShapeDtypeStruct((L, L), BF16)] * 4,
        compiler_params=_params(("parallel",), 40),
        name="dft_matrices",
    )(jnp.cos(pa), jnp.sin(pa), jnp.cos(pb), jnp.sin(pb))


def _dft_gen_kernel(ca_ref, sa_ref, cb_ref, sb_ref, cm_ref, sm_ref, icm_ref, ism_ref, *, n):
    tk, L = ca_ref.shape
    rows = min(NORM_ROWS, tk)
    cb = cb_ref[0]
    sb = sb_ref[0]
    k0 = pl.program_id(0) * tk
    col = lax.broadcasted_iota(jnp.int32, (rows, L), 1)
    alt_col = jnp.where((col & 1) == 0, 1.0, -1.0)

    def body(i, carry):
        r = pl.ds(pl.multiple_of(i * rows, rows), rows)
        ca = ca_ref[r, :]
        sa = sa_ref[r, :]
        cm = cb * ca - sb * sa
        sm = sb * ca + cb * sa
        row = k0 + i * rows + lax.broadcasted_iota(jnp.int32, (rows, L), 0)
        alt_row = jnp.where((row & 1) == 0, 1.0, -1.0)
        cm_ref[r, :] = cm.astype(BF16)
        sm_ref[r, :] = jnp.where(row == 0, alt_col, sm).astype(BF16)
        icm_ref[r, :] = jnp.where(col == 0, 1.0 / n, cm * (2.0 / n)).astype(BF16)
        ism_ref[r, :] = jnp.where(col == 0, alt_row * (1.0 / n), sm * (-2.0 / n)).astype(BF16)
        return carry

    lax.fori_loop(0, tk // rows, body, 0)


def _hy_positions(L, bands, width):
    t = jnp.arange(L, dtype=F32)
    t_unit = t / (L - 1)
    freqs = jnp.linspace(1e-4, bands - 1, bands, dtype=F32)
    ang = (2.0 * math.pi / L) * t[:, None] * freqs[None, :]
    z = jnp.concatenate([t_unit[:, None], jnp.cos(ang), jnp.sin(ang)], axis=-1)
    return jnp.pad(z, ((0, 0), (0, width - z.shape[1])))


def _log_sigmoid(x):
    return jnp.minimum(x, 0.0) - jnp.log(1.0 + jnp.exp(-jnp.abs(x)))


def _cumsum_lanes(x):
    lane = lax.broadcasted_iota(jnp.int32, x.shape, 1)
    sh = 1
    while sh < x.shape[1]:
        x = x + jnp.where(lane >= sh, pltpu.roll(x, sh, 1), 0.0)
        sh *= 2
    return x


def _mlstm_kernel(q_ref, k_ref, v_ref, o_ref, g_ref, hn_ref, y_ref,
                  qt_scr, vt_scr, ht_scr, bc_scr, st_scr, *, heads, hp):
    NC, Dh = qt_scr.shape[1], qt_scr.shape[2]
    AUG = vt_scr.shape[2]
    Lc = ML_CHUNK
    first = pl.program_id(1) * hp
    kscale = Dh ** -0.5
    ri = lax.broadcasted_iota(jnp.int32, (Lc, Lc), 0)
    ci = lax.broadcasted_iota(jnp.int32, (Lc, Lc), 1)
    valid = (ri <= ci, ri >= ci)
    chains = [(slot, direction) for slot in range(hp) for direction in (0, 1)]
    ones_row = jnp.where(lax.broadcasted_iota(jnp.int32, (AUG - Dh, Lc), 0) == 0, 1.0, 0.0).astype(BF16)

    for idx, (slot, direction) in enumerate(chains):
        logf = _log_sigmoid(g_ref[0, (2 * direction + 1) * heads + first + slot])
        pre = _cumsum_lanes(logf)
        if direction == 0:
            bc_scr[idx] = pre
        else:
            bc_scr[idx] = jnp.sum(logf, axis=1, keepdims=True) - pre + logf
        st_scr[idx] = jnp.zeros((AUG, Dh), F32)

    def transpose_in(n, carry):
        rows = pl.ds(pl.multiple_of(n * Lc, Lc), Lc)
        for slot in range(hp):
            cols = slice(slot * Dh, (slot + 1) * Dh)
            qt_scr[slot, n] = q_ref[0, rows, cols].astype(F32).T.astype(BF16)
            vt_scr[slot, n, 0:Dh, :] = v_ref[0, rows, cols].astype(F32).T.astype(BF16)
            vt_scr[slot, n, Dh:AUG, :] = ones_row
        return carry

    lax.fori_loop(0, NC, transpose_in, 0)

    def step(idx, slot, direction, n, m_st):
        cols = slice(slot * Dh, (slot + 1) * Dh)
        pos = pl.ds(pl.multiple_of(n * Lc, Lc), Lc)
        k = k_ref[0, pos, cols]
        qt = qt_scr[slot, n]
        vta = vt_scr[slot, n]
        state = st_scr[idx]
        bc_r = bc_scr[idx, pl.ds(n, 1), :]
        ig_r = g_ref[0, (2 * direction) * heads + first + slot, pl.ds(n, 1), :]
        g_tot = bc_r[:, Lc - 1:Lc] if direction == 0 else bc_r[:, 0:1]
        gmat = jnp.broadcast_to(ig_r - bc_r, (Lc, Lc)).T
        kq = _dot(k, qt)
        yield
        dmat = jnp.where(valid[direction], bc_r + gmat, -jnp.inf)
        inter = bc_r + m_st
        m_t = jnp.maximum(inter, jnp.max(dmat, axis=0, keepdims=True))
        st = (kq * (jnp.exp(dmat - m_t) * kscale)).astype(BF16)
        qte = (qt.astype(F32) * jnp.exp(inter - m_t)).astype(BF16)
        nd = _dot(jnp.concatenate([vta, state.astype(BF16)], axis=1),
                  jnp.concatenate([st, qte], axis=0))
        w_r = g_tot - bc_r + ig_r
        m_loc = jnp.max(w_r, axis=1, keepdims=True)
        e_w = jnp.exp(w_r - m_loc) * kscale
        loc = _dot((vta.astype(F32) * e_w).astype(BF16), k)
        yield
        den = nd[Dh:Dh + 1, :]
        ht_scr[direction, slot, n] = nd[0:Dh, :] / jnp.maximum(jnp.abs(den), jnp.exp(-m_t))
        m_new = jnp.maximum(g_tot + m_st, m_loc)
        st_scr[idx] = jnp.exp(g_tot + m_st - m_new) * state + jnp.exp(m_loc - m_new) * loc
        return m_new

    def body(i, carry):
        gens = [step(idx, slot, direction, i if direction == 0 else NC - 1 - i, carry[idx])
                for idx, (slot, direction) in enumerate(chains)]
        out = [None] * len(gens)
        while any(o is None for o in out):
            for idx, gen in enumerate(gens):
                if out[idx] is None:
                    try:
                        next(gen)
                    except StopIteration as done:
                        out[idx] = done.value
        return tuple(out)

    lax.fori_loop(0, NC, body, tuple(jnp.zeros((1, 1), F32) for _ in chains))

    hn_t = [jnp.broadcast_to(hn_ref[:, slot * Dh:(slot + 1) * Dh], (Lc, Dh)).T for slot in range(hp)]

    def combine(n, carry):
        pos = pl.ds(pl.multiple_of(n * Lc, Lc), Lc)
        for slot in range(hp):
            cols = slice(slot * Dh, (slot + 1) * Dh)
            hs = ht_scr[0, slot, n] + ht_scr[1, slot, n]
            hs = hs * lax.rsqrt(jnp.mean(hs * hs, axis=0, keepdims=True) + EPS) * hn_t[slot]
            y_ref[0, pos, cols] = (jax.nn.sigmoid(o_ref[0, pos, cols].astype(F32)) * hs.T).astype(BF16)
        return carry

    lax.fori_loop(0, NC, combine, 0)


def _mlstm(proj3, gate_rows, head_norm, C, heads):
    B, S, _ = proj3.shape
    Dh = C // heads
    NC = S // ML_CHUNK
    assert Dh == LANES, "the head dimension must fill one lane tile"
    hp = 2 if heads % 2 == 0 else 1
    aug = Dh + 16
    W = hp * Dh
    base = 3 * C // W
    col = lambda off: (lambda b, h: (b, 0, base + off * (heads // hp) + h))
    return pl.pallas_call(
        functools.partial(_mlstm_kernel, heads=heads, hp=hp),
        grid=(B, heads // hp),
        in_specs=[
            pl.BlockSpec((1, S, W), col(0)),
            pl.BlockSpec((1, S, W), col(1)),
            pl.BlockSpec((1, S, W), col(2)),
            pl.BlockSpec((1, S, W), col(3)),
            pl.BlockSpec((1, 4 * heads, NC, ML_CHUNK), lambda b, h: (b, 0, 0, 0)),
            pl.BlockSpec((1, W), lambda b, h: (0, h)),
        ],
        out_specs=pl.BlockSpec((1, S, W), lambda b, h: (b, 0, h)),
        out_shape=jax.ShapeDtypeStruct((B, S, C), BF16),
        scratch_shapes=[pltpu.VMEM((hp, NC, Dh, ML_CHUNK), BF16),
                        pltpu.VMEM((hp, NC, aug, ML_CHUNK), BF16),
                        pltpu.VMEM((2, hp, NC, Dh, ML_CHUNK), F32),
                        pltpu.VMEM((2 * hp, NC, ML_CHUNK), F32),
                        pltpu.VMEM((2 * hp, aug, Dh), F32)],
        compiler_params=_params(("parallel", "parallel"), 48),
        name="mlstm",
    )(proj3, proj3, proj3, proj3, gate_rows, head_norm)


def _cmul(ar, ai, br, bi):
    return ar * br - ai * bi, ar * bi + ai * br


def _slab_interleave(re, im, axis):
    axis = axis % re.ndim
    n = re.shape[axis]
    shp = re.shape[:axis] + (n // LANES, 1, LANES) + re.shape[axis + 1:]
    out = jnp.concatenate([re.reshape(shp), im.reshape(shp)], axis=axis + 1)
    return out.reshape(re.shape[:axis] + (2 * n,) + re.shape[axis + 1:])


def _s5_kernel(u_ref, w_ref, cm_ref, e_ref, dm_ref, lam_ref, d_ref, y_ref, yf_scr, yb_scr):
    S = yf_scr.shape[0]
    T = e_ref.shape[2]
    P2 = e_ref.shape[3]
    SL = 2 * LANES
    NCH = S // T
    ri = lax.broadcasted_iota(jnp.int32, (T, T), 0)
    ci = lax.broadcasted_iota(jnp.int32, (T, T), 1)
    tris = (jnp.where(ri >= ci, 1.0, 0.0).astype(BF16), jnp.where(ri <= ci, 1.0, 0.0).astype(BF16))
    outs = (yf_scr, yb_scr)

    UN = next(u for u in (S5_UNROLL, 2, 1) if NCH % u == 0)
    chains = [(d, k, c0) for d in (0, 1) for k in range(UN) for c0 in range(0, P2, SL)]

    def body(i, carry):
        rows = {}
        for k in range(UN):
            rows[0, k] = pl.ds(pl.multiple_of((i * UN + k) * T, T), T)
            rows[1, k] = pl.ds(pl.multiple_of((NCH - 1 - i * UN - k) * T, T), T)
        u = {key: u_ref[0, r, :] for key, r in rows.items()}
        x = [_dot(u[d, k], w_ref[d, 0, :, c0:c0 + SL]) for d, k, c0 in chains]
        xs = []
        for (d, k, c0), xv in zip(chains, x):
            sr, si = _cmul(e_ref[d, 0, :, c0:c0 + LANES], e_ref[d, 0, :, c0 + LANES:c0 + SL],
                           xv[:, :LANES], xv[:, LANES:])
            xs.append(jnp.concatenate([sr, si], axis=1).astype(BF16))
        cs = [_dot(tris[d], v) for (d, k, c0), v in zip(chains, xs)]
        hs = {}
        state = {(d, c0): carry[d][:, c0:c0 + SL] for d in (0, 1) for c0 in range(0, P2, SL)}
        for (d, k, c0), cv in zip(chains, cs):
            re = slice(c0, c0 + LANES)
            im = slice(c0 + LANES, c0 + SL)
            st = state[d, c0]
            br, bi = _cmul(lam_ref[d, 0, :, re], lam_ref[d, 0, :, im],
                           st[:, :LANES], st[:, LANES:])
            hr, hi = _cmul(dm_ref[d, 0, :, re], dm_ref[d, 0, :, im],
                           cv[:, :LANES] + br, cv[:, LANES:] + bi)
            h = jnp.concatenate([hr, hi], axis=1)
            last = T - 1 if d == 0 else 0
            state[d, c0] = h[last:last + 1, :]
            hs[d, k, c0] = h
        ys = {key: _dot(h.astype(BF16), cm_ref[0, key[2]:key[2] + SL, :]) for key, h in hs.items()}
        for (d, k), r in rows.items():
            outs[d][r, :] = sum(ys[d, k, c0] for c0 in range(0, P2, SL))
        return tuple(jnp.concatenate([state[d, c0] for c0 in range(0, P2, SL)], axis=1) for d in (0, 1))

    zero = jnp.zeros((1, P2), F32)
    lax.fori_loop(0, NCH // UN, body, (zero, zero))

    R = min(512, S)
    d = d_ref[...]

    def skip_gelu(i, carry):
        r = pl.ds(pl.multiple_of(i * R, R), R)
        y = yf_scr[r, :] + yb_scr[r, :] + d * u_ref[0, r, :].astype(F32)
        y_ref[0, r, :] = jax.nn.gelu(y).astype(BF16)
        return carry

    lax.fori_loop(0, S // R, skip_gelu, 0)


def _s5(proj3, w_bd, c_bd, e_tab, d_tab, lam_tab, d_skip, C):
    B, S, _ = proj3.shape
    NB, P2, U = c_bd.shape
    T = e_tab.shape[2]
    base = 7 * C // U
    return pl.pallas_call(
        _s5_kernel,
        grid=(B, NB),
        in_specs=[
            pl.BlockSpec((1, S, U), lambda b, j: (b, 0, base + j)),
            pl.BlockSpec((2, 1, U, P2), lambda b, j: (0, j, 0, 0)),
            pl.BlockSpec((1, P2, U), lambda b, j: (j, 0, 0)),
            pl.BlockSpec((2, 1, T, P2), lambda b, j: (0, j, 0, 0)),
            pl.BlockSpec((2, 1, T, P2), lambda b, j: (0, j, 0, 0)),
            pl.BlockSpec((2, 1, 1, P2), lambda b, j: (0, j, 0, 0)),
            pl.BlockSpec((1, U), lambda b, j: (0, j)),
        ],
        out_specs=pl.BlockSpec((1, S, U), lambda b, j: (b, 0, j)),
        out_shape=jax.ShapeDtypeStruct((B, S, C), BF16),
        scratch_shapes=[pltpu.VMEM((S, U), F32), pltpu.VMEM((S, U), F32)],
        compiler_params=_params(("parallel", "parallel"), 32),
        name="s5_scan",
    )(proj3, w_bd, c_bd, e_tab, d_tab, lam_tab, d_skip)


def _s5_tables(lam_re, lam_im, log_step, b_re, b_im, c_re, c_im):
    _, G, P = lam_re.shape
    Hg = b_re.shape[-1]
    gb = S5_BLOCK_GROUPS
    NB = G // gb
    T = S5_CHUNK
    lre = jnp.minimum(lam_re, -1e-4)
    lim = lam_im
    step = jnp.exp(log_step)[..., None]
    mag = jnp.exp(lre * step)
    ang = lim * step
    lbr = mag * jnp.cos(ang)
    lbi = mag * jnp.sin(ang)
    den = lre * lre + lim * lim
    qr = ((lbr - 1.0) * lre + lbi * lim) / den
    qi = (lbi * lre - (lbr - 1.0) * lim) / den
    bbr = qr[..., None] * b_re[None] - qi[..., None] * b_im[None]
    bbi = qr[..., None] * b_im[None] + qi[..., None] * b_re[None]
    eye = jnp.eye(gb, dtype=F32)

    def drive(bb):
        t = bb.reshape(2, NB, gb, P, Hg).transpose(0, 1, 2, 4, 3)
        return jnp.einsum("ab,djahp->djahbp", eye, t).reshape(2, NB, gb * Hg, gb * P)

    w_bd = _slab_interleave(drive(bbr), drive(bbi), -1).astype(BF16)

    def readout(cc):
        t = cc.reshape(NB, gb, Hg, P).transpose(0, 1, 3, 2)
        return jnp.einsum("ab,japh->japbh", eye, t).reshape(NB, gb * P, gb * Hg)

    c_bd = _slab_interleave(readout(c_re), -readout(c_im), 1).astype(BF16)

    def blocks(a):
        return a.reshape(2, -1, NB, gb * P).transpose(0, 2, 1, 3)

    r = jnp.arange(T, dtype=F32)
    r = jnp.stack([r, T - 1.0 - r])[:, :, None, None]
    lm = (lre * step)[:, None]
    an = ang[:, None]
    e_tab = _slab_interleave(blocks(jnp.exp(-r * lm) * jnp.cos(-r * an)),
                             blocks(jnp.exp(-r * lm) * jnp.sin(-r * an)), -1)
    d_tab = _slab_interleave(blocks(jnp.exp(r * lm) * jnp.cos(r * an)),
                             blocks(jnp.exp(r * lm) * jnp.sin(r * an)), -1)
    lam_tab = _slab_interleave(blocks(lbr[:, None]), blocks(lbi[:, None]), -1)
    return w_bd, c_bd, e_tab, d_tab, lam_tab


def _merge_kernel(ya_ref, yb_ref, yc_ref, gl_ref, gw_ref, gb_ref, wg_ref, bg_ref, wb_ref,
                  o_ref, yc_scr):
    @pl.when(pl.program_id(1) == 0)
    def _():
        y = yc_ref[...]
        gate = jax.nn.sigmoid(_dot(y, gw_ref[0]) + gb_ref[...])
        yc_scr[...] = (y.astype(F32) * gate).astype(BF16)

    gl = gl_ref[...].astype(BF16)
    acc = None
    for n, br in enumerate((ya_ref[...], yb_ref[...], yc_scr[...])):
        term = jax.nn.sigmoid(_dot(gl, wg_ref[0, n]) + bg_ref[n]) * _dot(br, wb_ref[0, n])
        acc = term if acc is None else acc + term
    o_ref[...] = acc.astype(BF16)


def _merge(ya, yb, yc, small, glu_w, glu_b, wg, bg, wb, layer):
    T, C = ya.shape
    _, NBR, R, D = wg.shape
    tm = _tile(T, TOKEN_TILE)
    tn = _tile(D, 512)
    row = lambda m, n: (m, 0)
    return pl.pallas_call(
        _merge_kernel,
        grid=(T // tm, D // tn),
        in_specs=[
            pl.BlockSpec((tm, C), row),
            pl.BlockSpec((tm, C), row),
            pl.BlockSpec((tm, C), row),
            pl.BlockSpec((tm, R), row),
            pl.BlockSpec((1, C, C), lambda m, n: (layer, 0, 0)),
            pl.BlockSpec((1, C), lambda m, n: (0, 0)),
            pl.BlockSpec((1, NBR, R, tn), lambda m, n: (layer, 0, 0, n)),
            pl.BlockSpec((NBR, 1, tn), lambda m, n: (0, 0, n)),
            pl.BlockSpec((1, NBR, C, tn), lambda m, n: (layer, 0, 0, n)),
        ],
        out_specs=pl.BlockSpec((tm, tn), lambda m, n: (m, n)),
        out_shape=jax.ShapeDtypeStruct((T, D), BF16),
        scratch_shapes=[pltpu.VMEM((tm, C), BF16)],
        compiler_params=_params(("parallel", "arbitrary"), 48),
        name="branch_merge",
    )(ya, yb, yc, small, glu_w, glu_b, wg, bg, wb)


def _proj_res_kernel(a_ref, w_ref, x_ref, g_ref, o_ref):
    o_ref[...] = x_ref[...] + g_ref[0] * _dot(a_ref[...], w_ref[0])


def _proj_residual(a, w, layer, x2, gate, S, tn_target, a_buffers, name):
    T, K = a.shape
    D = w.shape[2]
    tm = _tile(S, TOKEN_TILE)
    tn = _tile(D, tn_target)
    return pl.pallas_call(
        _proj_res_kernel,
        grid=(T // tm, D // tn),
        in_specs=[
            pl.BlockSpec((tm, K), lambda m, n: (m, 0), pipeline_mode=pl.Buffered(a_buffers)),
            pl.BlockSpec((1, K, tn), lambda m, n: (layer, 0, n)),
            pl.BlockSpec((tm, tn), lambda m, n: (m, n)),
            pl.BlockSpec((1, 1, tn), lambda m, n: ((m * tm) // S, 0, n)),
        ],
        out_specs=pl.BlockSpec((tm, tn), lambda m, n: (m, n)),
        out_shape=jax.ShapeDtypeStruct((T, D), F32),
        compiler_params=_params(("parallel", "parallel"), 52),
        name=name,
    )(a, w, x2, gate)


def _ffn_up_kernel(x_ref, nw_ref, sc_ref, sh_ref, wg_ref, wu_ref, o_ref, h_scr):
    @pl.when(pl.program_id(1) == 0)
    def _():
        _norm_mod_rows(x_ref, nw_ref, sc_ref, sh_ref, h_scr)

    h = h_scr[...]
    g = _dot(h, wg_ref[0])
    o_ref[...] = (g * jax.nn.sigmoid(g) * _dot(h, wu_ref[0])).astype(BF16)


def _ffn_up(x2, S, nw, scale, shift, wg, wu, layer):
    T, D = x2.shape
    N = wg.shape[2]
    tm = _tile(S, TOKEN_TILE)
    tn = _tile(N, 512)
    bidx = lambda m, n: ((m * tm) // S, 0, 0)
    return pl.pallas_call(
        _ffn_up_kernel,
        grid=(T // tm, N // tn),
        in_specs=[
            pl.BlockSpec((tm, D), lambda m, n: (m, 0), pipeline_mode=pl.Buffered(1)),
            pl.BlockSpec((1, D), lambda m, n: (0, 0)),
            pl.BlockSpec((1, 1, D), bidx),
            pl.BlockSpec((1, 1, D), bidx),
            pl.BlockSpec((1, D, tn), lambda m, n: (layer, 0, n)),
            pl.BlockSpec((1, D, tn), lambda m, n: (layer, 0, n)),
        ],
        out_specs=pl.BlockSpec((tm, tn), lambda m, n: (m, n)),
        out_shape=jax.ShapeDtypeStruct((T, N), BF16),
        scratch_shapes=[pltpu.VMEM((tm, D), BF16)],
        compiler_params=_params(("parallel", "arbitrary"), 52),
        name="swiglu_up",
    )(x2, nw, scale, shift, wg, wu)


def _final_norm_kernel(x_ref, w_ref, o_ref):
    rows = min(NORM_ROWS, x_ref.shape[0])
    w = w_ref[...]

    def body(i, carry):
        r = pl.ds(pl.multiple_of(i * rows, rows), rows)
        x = x_ref[r, :]
        o_ref[r, :] = x * lax.rsqrt(jnp.mean(x * x, axis=-1, keepdims=True) + EPS) * w
        return carry

    lax.fori_loop(0, x_ref.shape[0] // rows, body, 0)


def _final_norm(x2, w):
    T, D = x2.shape
    tm = _tile(T, 256)
    return pl.pallas_call(
        _final_norm_kernel,
        grid=(T // tm,),
        in_specs=[pl.BlockSpec((tm, D), lambda m: (m, 0)), pl.BlockSpec((1, D), lambda m: (0, 0))],
        out_specs=pl.BlockSpec((tm, D), lambda m: (m, 0)),
        out_shape=jax.ShapeDtypeStruct((T, D), F32),
        compiler_params=_params(("parallel",), 40),
        name="final_norm",
    )(x2, w)


def _pad_to(a, axis, size):
    pad = [(0, 0)] * a.ndim
    pad[axis] = (0, size - a.shape[axis])
    return jnp.pad(a, pad)


def kernel(x, c, w_cond, w_mod, b_mod, norm_mix, norm_ffn, w_in, b_mgate, hy_conv_w, hy_conv_b, hy_f_w1, hy_f_b1, hy_f_w2, hy_f_b2, hy_f_w3, hy_sin_freq, hy_decay, hy_bias, ml_norm, s5_lam_re, s5_lam_im, s5_log_step, s5_b_re, s5_b_im, s5_c_re, s5_c_im, s5_d, s5_glu_w, s5_glu_b, w_gate_up, b_gate, w_branch, w_out, w_ffn_gate, w_ffn_up, w_ffn_down, norm_final):
    B, S, D = x.shape
    depth = w_in.shape[0]
    C = D // 4
    T = B * S
    n_gate = b_mgate.shape[-1]
    heads = n_gate // 4
    R = w_gate_up.shape[2]
    NS = -(-(R + n_gate) // LANES) * LANES
    NC = S // ML_CHUNK
    bands = (hy_f_w1.shape[1] - 1) // 2
    F = hy_f_w1.shape[2]
    Fp = -(-F // LANES) * LANES

    mod = _modulation(c, w_cond, w_mod, b_mod).reshape(depth, B, 6, 1, D)
    cm, sm, icm, ism = _dft_matrices(S)
    zpos = _hy_positions(S, bands, LANES)
    glu_w16, w_gate16, w_branch16, w_out16, w_ffg16, w_ffu16, w_ffd16 = (
        w.astype(BF16) for w in (s5_glu_w, w_gate_up, w_branch, w_out, w_ffn_gate, w_ffn_up, w_ffn_down))

    x2 = x.reshape(T, D)
    for l in range(depth):
        shift1, scale1, gate1, shift2, scale2, gate2 = (mod[l, :, i] for i in range(6))

        wl = w_in[l]
        w_main = jnp.concatenate([wl[:, :7 * C], wl[:, 7 * C + n_gate:8 * C + n_gate]], axis=1)
        w_small = jnp.concatenate([wl[:, 8 * C + n_gate:], wl[:, 7 * C:7 * C + n_gate]], axis=1)
        w_small = _pad_to(w_small, 1, NS)
        b_small = _pad_to(jnp.concatenate([jnp.zeros((R,), F32), b_mgate[l]]), 0, NS).reshape(1, NS)
        proj, small = _inproj(x2, S, norm_mix[l].reshape(1, D), scale1, shift1,
                              w_main.astype(BF16), w_small.astype(BF16), b_small)
        proj3 = proj.reshape(B, S, 8 * C)

        u3 = _hy_pre(proj3, hy_conv_w[l], hy_conv_b[l], C)
        kp, km, kn = _hy_filters(
            zpos, _pad_to(_pad_to(hy_f_w1[l], 0, LANES), 1, Fp), _pad_to(hy_f_b1[l], 0, Fp).reshape(1, Fp),
            _pad_to(_pad_to(hy_f_w2[l], 0, Fp), 1, Fp), _pad_to(hy_f_b2[l], 0, Fp).reshape(1, Fp),
            _pad_to(hy_sin_freq[l], 1, Fp), _pad_to(hy_f_w3[l], 0, Fp),
            hy_decay[l].reshape(1, -1), C)
        kr, ki = _hy_filter_spectrum(cm, sm, kp, km, kn)
        hy_b = hy_bias[l].reshape(HY_ORDER, 1, C)
        zarr, zcol = u3, 2 * C
        for o in range(HY_ORDER):
            pr, pi = _hy_spectrum_product(cm, sm, zarr, zcol, kr, ki, o, C)
            zarr = _hy_inverse_gate(icm, ism, pr, pi, u3, zarr, zcol, hy_b, o, C)
            zcol = 0
        y_a = zarr.reshape(T, C)

        gate_rows = small[:, R:R + n_gate].reshape(B, NC, ML_CHUNK, n_gate).transpose(0, 3, 1, 2)
        y_b = _mlstm(proj3, gate_rows, ml_norm[l].reshape(1, C), C, heads).reshape(T, C)

        w_bd, c_bd, e_tab, d_tab, lam_tab = _s5_tables(
            s5_lam_re[l], s5_lam_im[l], s5_log_step[l], s5_b_re[l], s5_b_im[l], s5_c_re[l], s5_c_im[l])
        y_c = _s5(proj3, w_bd, c_bd, e_tab, d_tab, lam_tab, s5_d[l].reshape(1, C), C).reshape(T, C)

        merged = _merge(y_a, y_b, y_c, small, glu_w16, s5_glu_b[l].reshape(1, C),
                        w_gate16, b_gate[l].reshape(-1, 1, D), w_branch16, l)
        x2 = _proj_residual(merged, w_out16, l, x2, gate1, S, 512, 2, "out_projection")

        hidden = _ffn_up(x2, S, norm_ffn[l].reshape(1, D), scale2, shift2, w_ffg16, w_ffu16, l)
        x2 = _proj_residual(hidden, w_ffd16, l, x2, gate2, S, 256, 1, "swiglu_down")

    return _final_norm(x2, norm_final.reshape(1, D)).reshape(B, S, D)
```

```python
import functools
import math

import jax
import jax.numpy as jnp
from jax import lax
from jax.experimental import pallas as pl
from jax.experimental.pallas import tpu as pltpu

F32 = jnp.float32
BF16 = jnp.bfloat16
EPS = 1e-6
HIGHEST = lax.Precision.HIGHEST

LANES = 128
HY_ORDER = 2
ML_CHUNK = 128
S5_CHUNK = 64
S5_BLOCK_GROUPS = 8
S5_UNROLL = 2
NORM_ROWS = 32
TOKEN_TILE = 1024
MIB = 1024 * 1024


def _tile(dim, target, align=LANES):
    if dim <= target:
        return dim
    t = (target // align) * align
    while t >= align:
        if dim % t == 0:
            return t
        t -= align
    return dim


def _params(sem, vmem_mib):
    return pltpu.CompilerParams(dimension_semantics=sem, vmem_limit_bytes=vmem_mib * MIB)


def _dot(a, b):
    return jnp.dot(a, b, preferred_element_type=F32)


def _dot_hi(a, b):
    return jnp.dot(a, b, preferred_element_type=F32, precision=HIGHEST)


def _mod_kernel(c_ref, wc_ref, wm_ref, bm_ref, o_ref):
    c = c_ref[...]
    cond_h = _dot_hi(c * jax.nn.sigmoid(c), wc_ref[...])
    o_ref[0] = _dot_hi(cond_h, wm_ref[0]) + bm_ref[0]


def _modulation(c, w_cond, w_mod, b_mod):
    B, D = c.shape
    depth, R, W = w_mod.shape
    rows = max(8, B)
    cp = jnp.zeros((rows, D), F32).at[:B].set(c)
    tn = _tile(W, 4096)
    out = pl.pallas_call(
        _mod_kernel,
        grid=(depth, W // tn),
        in_specs=[
            pl.BlockSpec((rows, D), lambda l, n: (0, 0)),
            pl.BlockSpec((D, R), lambda l, n: (0, 0)),
            pl.BlockSpec((1, R, tn), lambda l, n: (l, 0, n)),
            pl.BlockSpec((1, 1, tn), lambda l, n: (l, 0, n)),
        ],
        out_specs=pl.BlockSpec((1, rows, tn), lambda l, n: (l, 0, n)),
        out_shape=jax.ShapeDtypeStruct((depth, rows, W), F32),
        compiler_params=_params(("arbitrary", "arbitrary"), 40),
        name="adaln_modulation",
    )(cp, w_cond, w_mod, b_mod.reshape(depth, 1, W))
    return out[:, :B]


def _norm_mod(x, nw, scale, shift):
    ms = jnp.mean(x * x, axis=-1, keepdims=True)
    y = x * lax.rsqrt(ms + EPS) * nw
    return y * (1.0 + scale) + shift


def _norm_mod_rows(x_ref, nw_ref, sc_ref, sh_ref, h_scr):
    rows = min(NORM_ROWS, x_ref.shape[0])
    nw = nw_ref[...]
    scale = sc_ref[0]
    shift = sh_ref[0]

    def body(i, carry):
        r = pl.ds(pl.multiple_of(i * rows, rows), rows)
        h_scr[r, :] = _norm_mod(x_ref[r, :], nw, scale, shift).astype(BF16)
        return carry

    lax.fori_loop(0, x_ref.shape[0] // rows, body, 0)


def _inproj_kernel(x_ref, nw_ref, sc_ref, sh_ref, w_ref, ws_ref, bs_ref, o_ref, os_ref, h_scr):
    @pl.when(pl.program_id(1) == 0)
    def _():
        _norm_mod_rows(x_ref, nw_ref, sc_ref, sh_ref, h_scr)
        os_ref[...] = _dot(h_scr[...], ws_ref[...]) + bs_ref[...]

    o_ref[...] = _dot(h_scr[...], w_ref[...]).astype(BF16)


def _inproj(x2, S, nw, scale, shift, w_main, w_small, b_small):
    T, D = x2.shape
    N = w_main.shape[1]
    NS = w_small.shape[1]
    tm = _tile(S, TOKEN_TILE)
    tn = _tile(N, 512)
    bidx = lambda m, n: ((m * tm) // S, 0, 0)
    return pl.pallas_call(
        _inproj_kernel,
        grid=(T // tm, N // tn),
        in_specs=[
            pl.BlockSpec((tm, D), lambda m, n: (m, 0), pipeline_mode=pl.Buffered(1)),
            pl.BlockSpec((1, D), lambda m, n: (0, 0)),
            pl.BlockSpec((1, 1, D), bidx),
            pl.BlockSpec((1, 1, D), bidx),
            pl.BlockSpec((D, tn), lambda m, n: (0, n)),
            pl.BlockSpec((D, NS), lambda m, n: (0, 0)),
            pl.BlockSpec((1, NS), lambda m, n: (0, 0)),
        ],
        out_specs=[
            pl.BlockSpec((tm, tn), lambda m, n: (m, n)),
            pl.BlockSpec((tm, NS), lambda m, n: (m, 0)),
        ],
        out_shape=[
            jax.ShapeDtypeStruct((T, N), BF16),
            jax.ShapeDtypeStruct((T, NS), F32),
        ],
        scratch_shapes=[pltpu.VMEM((tm, D), BF16)],
        compiler_params=_params(("parallel", "arbitrary"), 56),
        name="in_projection",
    )(x2, nw, scale, shift, w_main, w_small, b_small)


def _hy_pre_kernel(u_ref, w_ref, b_ref, o_ref):
    S, tn = u_ref.shape[1], u_ref.shape[2]
    R = min(256, S)
    halo = 16
    nchunks = S // R
    w = w_ref[...]
    b = b_ref[...]
    row = lax.broadcasted_iota(jnp.int32, (R, tn), 0)

    def body(i, carry):
        r0 = pl.multiple_of(i * R, R)
        x = u_ref[0, pl.ds(r0, R), :].astype(F32)
        lo = pl.multiple_of(jnp.maximum(r0 - halo, 0), halo)
        hi = pl.multiple_of(jnp.minimum(r0 + R, S - halo), halo)
        before = u_ref[0, pl.ds(lo, halo), :].astype(F32)[halo - 1:halo]
        after = u_ref[0, pl.ds(hi, halo), :].astype(F32)[0:1]
        before = jnp.where(i == 0, 0.0, before)
        after = jnp.where(i == nchunks - 1, 0.0, after)
        prev = jnp.where(row == 0, before, pltpu.roll(x, 1, 0))
        nxt = jnp.where(row == R - 1, after, pltpu.roll(x, R - 1, 0))
        y = prev * w[0:1] + x * w[1:2] + nxt * w[2:3] + b
        o_ref[0, pl.ds(r0, R), :] = y.astype(BF16)
        return carry

    lax.fori_loop(0, nchunks, body, 0)


def _hy_pre(proj3, conv_w, conv_b, C):
    B, S, _ = proj3.shape
    W = 3 * C
    tn = _tile(W, 512)
    return pl.pallas_call(
        _hy_pre_kernel,
        grid=(B, W // tn),
        in_specs=[
            pl.BlockSpec((1, S, tn), lambda b, j: (b, 0, j)),
            pl.BlockSpec((3, tn), lambda b, j: (0, j)),
            pl.BlockSpec((1, tn), lambda b, j: (0, j)),
        ],
        out_specs=pl.BlockSpec((1, S, tn), lambda b, j: (b, 0, j)),
        out_shape=jax.ShapeDtypeStruct((B, S, W), BF16),
        compiler_params=_params(("parallel", "parallel"), 48),
        name="hyena_depthwise_conv",
    )(proj3, conv_w, conv_b.reshape(1, W))


def _hy_filter_kernel(z_ref, w1_ref, b1_ref, w2_ref, b2_ref, sf_ref, w3f_ref, w3b_ref,
                      df_ref, db_ref, kp_ref, km_ref, kn_ref, ff_scr, fb_scr, h_scr):
    L, tc = ff_scr.shape
    R = min(256, L)
    sf = sf_ref[...]
    dec_f = jnp.abs(df_ref[...])
    dec_b = jnp.abs(db_ref[...])
    row = lax.broadcasted_iota(jnp.int32, (R, tc), 0)
    sign = jnp.where((row & 1) == 0, 1.0, -1.0)

    @pl.when(jnp.logical_and(pl.program_id(0) == 0, pl.program_id(1) == 0))
    def _():
        def hidden(i, carry):
            r = pl.ds(pl.multiple_of(i * R, R), R)
            h = jnp.sin(sf[0:1] * (_dot_hi(z_ref[r, :], w1_ref[...]) + b1_ref[...]))
            h_scr[r, :] = jnp.sin(sf[1:2] * (_dot_hi(h, w2_ref[...]) + b2_ref[...]))
            return carry

        lax.fori_loop(0, L // R, hidden, 0)

    def windowed(i, carry):
        ss, kn = carry
        r = pl.ds(pl.multiple_of(i * R, R), R)
        t_unit = z_ref[r, 0:1]
        h = h_scr[r, :]
        ff = _dot_hi(h, w3f_ref[...]) * jnp.exp(-t_unit * dec_f)
        fb = _dot_hi(h, w3b_ref[...]) * jnp.exp(-t_unit * dec_b)
        fb = jnp.where(jnp.logical_and(row == 0, i == 0), 0.0, fb)
        ff_scr[r, :] = ff
        fb_scr[r, :] = fb
        ss = ss + jnp.sum(ff * ff, axis=0, keepdims=True) + jnp.sum(fb * fb, axis=0, keepdims=True)
        kn = kn + jnp.sum((ff + fb) * sign, axis=0, keepdims=True)
        return ss, kn

    zero = jnp.zeros((1, tc), F32)
    ss, kn = lax.fori_loop(0, L // R, windowed, (zero, zero))
    scale = lax.rsqrt(ss + EPS)
    kn_ref[...] = kn * scale

    def normalised(i, carry):
        r = pl.ds(pl.multiple_of(i * R, R), R)
        ff = ff_scr[r, :]
        fb = fb_scr[r, :]
        kp_ref[r, :] = ((ff + fb) * scale).astype(BF16)
        km_ref[r, :] = ((fb - ff) * scale).astype(BF16)
        return carry

    lax.fori_loop(0, L // R, normalised, 0)


def _hy_filters(zpos, w1, b1, w2, b2, sf, w3, decay, C):
    L, E = zpos.shape
    F = w1.shape[1]
    tc = _tile(C, 256)
    nt = C // tc
    fwd = lambda o, j: (0, o * nt + j)
    bwd = lambda o, j: (0, (HY_ORDER + o) * nt + j)
    const = lambda o, j: (0, 0)
    return pl.pallas_call(
        _hy_filter_kernel,
        grid=(HY_ORDER, nt),
        in_specs=[
            pl.BlockSpec((L, E), const),
            pl.BlockSpec((E, F), const),
            pl.BlockSpec((1, F), const),
            pl.BlockSpec((F, F), const),
            pl.BlockSpec((1, F), const),
            pl.BlockSpec((2, F), const),
            pl.BlockSpec((F, tc), fwd),
            pl.BlockSpec((F, tc), bwd),
            pl.BlockSpec((1, tc), fwd),
            pl.BlockSpec((1, tc), bwd),
        ],
        out_specs=[
            pl.BlockSpec((L, tc), fwd),
            pl.BlockSpec((L, tc), fwd),
            pl.BlockSpec((1, tc), fwd),
        ],
        out_shape=[
            jax.ShapeDtypeStruct((L, HY_ORDER * C), BF16),
            jax.ShapeDtypeStruct((L, HY_ORDER * C), BF16),
            jax.ShapeDtypeStruct((1, HY_ORDER * C), F32),
        ],
        scratch_shapes=[pltpu.VMEM((L, tc), F32), pltpu.VMEM((L, tc), F32), pltpu.VMEM((L, F), F32)],
        compiler_params=_params(("arbitrary", "arbitrary"), 40),
        name="hyena_filters",
    )(zpos, w1, b1, w2, b2, sf, w3, w3, decay, decay)


def _rollflip(v):
    return jnp.concatenate([v[:, :1], jnp.flip(v[:, 1:], axis=1)], axis=1)


def _fold_kernel(a_ref, r_ref, p_ref, m_ref, mid_ref, nyq_ref):
    H, tn = p_ref.shape[1], p_ref.shape[2]
    R = min(256, H)
    row = lax.broadcasted_iota(jnp.int32, (R, tn), 0)
    sign = jnp.where((row & 1) == 0, 1.0, -1.0)

    def body(i, nyq):
        rows = pl.ds(pl.multiple_of(i * R, R), R)
        a = a_ref[0, rows, :].astype(F32)
        rr = r_ref[0, rows, :].astype(F32)
        first = jnp.logical_and(row == 0, i == 0)
        p = jnp.where(first, a, a + rr)
        p_ref[0, rows, :] = p.astype(BF16)
        m_ref[0, rows, :] = jnp.where(first, a, a - rr).astype(BF16)
        return nyq + jnp.sum(p * sign, axis=0, keepdims=True)

    nyq = lax.fori_loop(0, H // R, body, jnp.zeros((1, tn), F32))
    mid = r_ref[0, 0:1, :].astype(F32)
    mid_ref[0] = mid
    nyq_ref[0] = nyq + (1.0 if H % 2 == 0 else -1.0) * mid


def _hy_fold(a_arr, a_col, r_arr, r_col, W):
    B, H, _ = r_arr.shape
    tn = _tile(W, 512)
    aoff, roff = a_col // tn, r_col // tn
    half = pl.BlockSpec((1, H, tn), lambda b, j: (b, 0, j))
    rowv = pl.BlockSpec((1, 1, tn), lambda b, j: (b, 0, j))
    return pl.pallas_call(
        _fold_kernel,
        grid=(B, W // tn),
        in_specs=[pl.BlockSpec((1, H, tn), lambda b, j: (b, 0, aoff + j)),
                  pl.BlockSpec((1, H, tn), lambda b, j: (b, 0, roff + j))],
        out_specs=[half, half, rowv, rowv],
        out_shape=[jax.ShapeDtypeStruct((B, H, W), BF16)] * 2 + [jax.ShapeDtypeStruct((B, 1, W), F32)] * 2,
        compiler_params=_params(("parallel", "parallel"), 40),
        name="hyena_fold",
    )(a_arr, r_arr)


def _half_spectra(ce_ref, se_ref, co_ref, so_ref, cp, cm_, cmid, sp, sm_, smid, nyq):
    tk = ce_ref.shape[0]
    row = pl.program_id(0) * tk + lax.broadcasted_iota(jnp.int32, (tk, cp.shape[1]), 0)
    sgn = jnp.where((row & 1) == 0, 1.0, -1.0)
    re_e = _dot(ce_ref[...], cp) + sgn * cmid
    re_o = _dot(co_ref[...], cm_)
    si_e = jnp.where(row == 0, nyq, _dot(se_ref[...], sm_))
    si_o = _dot(so_ref[...], sp) + sgn * smid
    return re_e, si_e, re_o, si_o, row == 0


def _fspec_filter_kernel(ce_ref, se_ref, co_ref, so_ref, pp_ref, pm_ref, pmid_ref, pnyq_ref,
                         mp_ref, mm_ref, mmid_ref, kre_ref, kie_ref, kro_ref, kio_ref):
    re_e, si_e, re_o, si_o, _ = _half_spectra(
        ce_ref, se_ref, co_ref, so_ref, pp_ref[0], pm_ref[0], pmid_ref[0],
        mp_ref[0], mm_ref[0], mmid_ref[0], pnyq_ref[0])
    kre_ref[...] = re_e
    kie_ref[...] = si_e
    kro_ref[...] = re_o
    kio_ref[...] = si_o


def _hy_filter_spectrum_folded(mats, pfold, mfold):
    ce, se, co, so = mats[:4]
    pp, pm, pmid, pnyq = pfold
    mp, mm, mmid, _ = mfold
    _, H, W = pp.shape
    tk = _tile(H, 512)
    tn = _tile(W, 512)
    mat = pl.BlockSpec((tk, H), lambda k, j: (k, 0))
    half = pl.BlockSpec((1, H, tn), lambda k, j: (0, 0, j))
    rowv = pl.BlockSpec((1, 1, tn), lambda k, j: (0, 0, j))
    out = pl.BlockSpec((tk, tn), lambda k, j: (k, j))
    return pl.pallas_call(
        _fspec_filter_kernel,
        grid=(H // tk, W // tn),
        in_specs=[mat] * 4 + [half, half, rowv, rowv, half, half, rowv],
        out_specs=[out] * 4,
        out_shape=[jax.ShapeDtypeStruct((H, W), F32)] * 4,
        compiler_params=_params(("parallel", "parallel"), 52),
        name="hyena_filter_spectrum",
    )(ce, se, co, so, pp, pm, pmid, pnyq, mp, mm, mmid)


def _fspec_data_kernel(ce_ref, se_ref, co_ref, so_ref, zp_ref, zm_ref, mid_ref, nyq_ref,
                       kre_ref, kie_ref, kro_ref, kio_ref, pre_ref, pie_ref, pro_ref, pio_ref):
    zp, zm, mid = zp_ref[0], zm_ref[0], mid_ref[0]
    zre, zse, zro, zso, first = _half_spectra(ce_ref, se_ref, co_ref, so_ref,
                                              zp, zm, mid, zp, zm, mid, nyq_ref[0])
    kre, kie, kro, kio = kre_ref[...], kie_ref[...], kro_ref[...], kio_ref[...]
    t = zse * kie
    pre_ref[0] = (zre * kre + jnp.where(first, 0.0, t)).astype(BF16)
    pie_ref[0] = jnp.where(first, t, zre * kie - zse * kre).astype(BF16)
    pro_ref[0] = (zro * kro + zso * kio).astype(BF16)
    pio_ref[0] = (zro * kio - zso * kro).astype(BF16)


def _hy_spectrum_product_folded(mats, zfold, kspec, order, C):
    ce, se, co, so = mats[:4]
    zp, zm, mid, nyq = zfold
    B, H, _ = zp.shape
    tk = _tile(H, 512)
    tn = _tile(C, 512)
    nt = C // tn
    mat = pl.BlockSpec((tk, H), lambda k, b, j: (k, 0))
    half = pl.BlockSpec((1, H, tn), lambda k, b, j: (b, 0, j))
    rowv = pl.BlockSpec((1, 1, tn), lambda k, b, j: (b, 0, j))
    kblk = pl.BlockSpec((tk, tn), lambda k, b, j: (k, order * nt + j))
    out = pl.BlockSpec((1, tk, tn), lambda k, b, j: (b, k, j))
    return pl.pallas_call(
        _fspec_data_kernel,
        grid=(H // tk, B, nt),
        in_specs=[mat] * 4 + [half, half, rowv, rowv] + [kblk] * 4,
        out_specs=[out] * 4,
        out_shape=[jax.ShapeDtypeStruct((B, H, C), BF16)] * 4,
        compiler_params=_params(("parallel", "parallel", "parallel"), 52),
        name="hyena_spectrum_product",
    )(ce, se, co, so, zp, zm, mid, nyq, *kspec)


def _finv_kernel(ice_ref, ise_ref, ico_ref, iso_ref, pre_ref, pie_ref, pro_ref, pio_ref,
                 xa_ref, xr_ref, za_ref, zr_ref, b_ref, oa_ref, or_ref, mid_scr, *, n):
    tt, H = ice_ref.shape
    tn = oa_ref.shape[2]
    tb = pl.program_id(0)
    pnyq = pie_ref[0, 0:1, :].astype(F32)

    @pl.when(tb == 0)
    def _():
        R = min(256, H)
        row = lax.broadcasted_iota(jnp.int32, (R, tn), 0)
        sign = jnp.where((row & 1) == 0, 1.0, -1.0)

        def body(i, acc):
            rows = pl.ds(pl.multiple_of(i * R, R), R)
            d = pre_ref[0, rows, :].astype(F32) - pio_ref[0, rows, :].astype(F32)
            return acc + jnp.sum(d * sign, axis=0, keepdims=True)

        acc = lax.fori_loop(0, H // R, body, jnp.zeros((1, tn), F32))
        dc = pre_ref[0, 0:1, :].astype(F32)
        mid_scr[...] = (2.0 * acc - dc + (1.0 if H % 2 == 0 else -1.0) * pnyq) * (1.0 / n)

    ec = _dot(ice_ref[...], pre_ref[0])
    es = _dot(ise_ref[...], pie_ref[0])
    oc = _dot(ico_ref[...], pro_ref[0])
    os_ = _dot(iso_ref[...], pio_ref[0])
    row = tb * tt + lax.broadcasted_iota(jnp.int32, (tt, tn), 0)
    nyq_t = jnp.where((row & 1) == 0, 1.0 / n, -1.0 / n) * pnyq
    y1 = ec - es + oc - os_ + nyq_t
    y2 = jnp.where(row == 0, mid_scr[...], ec + es - oc - os_ + nyq_t)
    bias = b_ref[0]
    oa_ref[0] = (xa_ref[0].astype(F32) * (y1 + za_ref[0].astype(F32) * bias)).astype(BF16)
    or_ref[0] = (xr_ref[0].astype(F32) * (y2 + zr_ref[0].astype(F32) * bias)).astype(BF16)


def _hy_inverse_gate_folded(mats, prod, u3, u3r, za_arr, za_col, zr_arr, zr_col, bias, order, C):
    ice, ise, ico, iso = mats[4:]
    B, H, _ = prod[0].shape
    tt = _tile(H, 512)
    tn = _tile(C, 512)
    nt = C // tn
    aoff, roff = za_col // tn, zr_col // tn
    mat = pl.BlockSpec((tt, H), lambda t, b, j: (t, 0))
    half = pl.BlockSpec((1, H, tn), lambda t, b, j: (b, 0, j))
    gate = pl.BlockSpec((1, tt, tn), lambda t, b, j: (b, t, order * nt + j))
    out = pl.BlockSpec((1, tt, tn), lambda t, b, j: (b, t, j))
    return pl.pallas_call(
        functools.partial(_finv_kernel, n=4 * H),
        grid=(H // tt, B, nt),
        in_specs=[mat] * 4 + [half] * 4 + [
            gate, gate,
            pl.BlockSpec((1, tt, tn), lambda t, b, j: (b, t, aoff + j)),
            pl.BlockSpec((1, tt, tn), lambda t, b, j: (b, t, roff + j)),
            pl.BlockSpec((1, 1, tn), lambda t, b, j: (order, 0, j)),
        ],
        out_specs=[out, out],
        out_shape=[jax.ShapeDtypeStruct((B, H, C), BF16)] * 2,
        scratch_shapes=[pltpu.VMEM((1, tn), F32)],
        compiler_params=_params(("arbitrary", "arbitrary", "arbitrary"), 52),
        name="hyena_inverse_gate",
    )(ice, ise, ico, iso, *prod, u3, u3r, za_arr, zr_arr, bias)


def _folded_dft_matrices(L):
    H = L // 2
    N = 2 * L
    tk = _tile(H, 128, 8)
    nb = H // tk
    idx = jnp.arange(H, dtype=jnp.int32)
    loc = jnp.arange(tk, dtype=jnp.int32)
    blk = jnp.arange(nb, dtype=jnp.int32) * tk

    def trig(mult_rows, mult_cols):
        ph = ((mult_rows[:, None] * mult_cols[None, :]) % N).astype(F32) * (2.0 * math.pi / N)
        return jnp.cos(ph), jnp.sin(ph)

    tables = [*trig(2 * loc, idx), *trig(2 * loc + 1, idx), *trig(loc, 2 * idx + 1)]
    bases = [a.reshape(nb, 1, H) for a in (*trig(2 * blk, idx), *trig(blk, 2 * idx + 1))]
    table = pl.BlockSpec((tk, H), lambda k: (0, 0))
    base = pl.BlockSpec((1, 1, H), lambda k: (k, 0, 0))
    out = pl.BlockSpec((tk, H), lambda k: (k, 0))
    return pl.pallas_call(
        functools.partial(_folded_dft_gen_kernel, n=N),
        grid=(nb,),
        in_specs=[table] * 6 + [base] * 4,
        out_specs=[out] * 8,
        out_shape=[jax.ShapeDtypeStruct((H, H), BF16)] * 8,
        compiler_params=_params(("parallel",), 40),
        name="dft_matrices",
    )(*tables, *bases)


def _folded_dft_gen_kernel(tec_ref, tes_ref, toc_ref, tos_ref, ttc_ref, tts_ref,
                           bec_ref, bes_ref, btc_ref, bts_ref,
                           ce_ref, se_ref, co_ref, so_ref, ice_ref, ise_ref, ico_ref, iso_ref, *, n):
    tk, H = tec_ref.shape
    rows = min(NORM_ROWS, tk)
    bec, bes, btc, bts = bec_ref[0], bes_ref[0], btc_ref[0], bts_ref[0]
    col = lax.broadcasted_iota(jnp.int32, (rows, H), 1)
    w = 2.0 / n

    def body(i, carry):
        r = pl.ds(pl.multiple_of(i * rows, rows), rows)
        tec, tes = tec_ref[r, :], tes_ref[r, :]
        toc, tos = toc_ref[r, :], tos_ref[r, :]
        ttc, tts = ttc_ref[r, :], tts_ref[r, :]
        ce = bec * tec - bes * tes
        se = bes * tec + bec * tes
        ce_ref[r, :] = ce.astype(BF16)
        se_ref[r, :] = se.astype(BF16)
        co_ref[r, :] = (bec * toc - bes * tos).astype(BF16)
        so_ref[r, :] = (bes * toc + bec * tos).astype(BF16)
        ice_ref[r, :] = jnp.where(col == 0, 1.0 / n, ce * w).astype(BF16)
        ise_ref[r, :] = (se * w).astype(BF16)
        ico_ref[r, :] = ((btc * ttc - bts * tts) * w).astype(BF16)
        iso_ref[r, :] = ((bts * ttc + btc * tts) * w).astype(BF16)
        return carry

    lax.fori_loop(0, tk // rows, body, 0)


def _hy_positions(L, bands, width):
    t = jnp.arange(L, dtype=F32)
    t_unit = t / (L - 1)
    freqs = jnp.linspace(1e-4, bands - 1, bands, dtype=F32)
    ang = (2.0 * math.pi / L) * t[:, None] * freqs[None, :]
    z = jnp.concatenate([t_unit[:, None], jnp.cos(ang), jnp.sin(ang)], axis=-1)
    return jnp.pad(z, ((0, 0), (0, width - z.shape[1])))


def _log_sigmoid(x):
    return jnp.minimum(x, 0.0) - jnp.log(1.0 + jnp.exp(-jnp.abs(x)))


def _cumsum_lanes(x):
    lane = lax.broadcasted_iota(jnp.int32, x.shape, 1)
    sh = 1
    while sh < x.shape[1]:
        x = x + jnp.where(lane >= sh, pltpu.roll(x, sh, 1), 0.0)
        sh *= 2
    return x


def _mlstm_kernel(q_ref, k_ref, v_ref, o_ref, g_ref, hn_ref, y_ref,
                  qt_scr, vt_scr, ht_scr, bc_scr, st_scr, *, heads, hp):
    NC, Dh = qt_scr.shape[1], qt_scr.shape[2]
    AUG = vt_scr.shape[2]
    Lc = ML_CHUNK
    first = pl.program_id(1) * hp
    kscale = Dh ** -0.5
    ri = lax.broadcasted_iota(jnp.int32, (Lc, Lc), 0)
    ci = lax.broadcasted_iota(jnp.int32, (Lc, Lc), 1)
    valid = (ri <= ci, ri >= ci)
    chains = [(slot, direction) for slot in range(hp) for direction in (0, 1)]
    ones_row = jnp.where(lax.broadcasted_iota(jnp.int32, (AUG - Dh, Lc), 0) == 0, 1.0, 0.0).astype(BF16)

    for idx, (slot, direction) in enumerate(chains):
        logf = _log_sigmoid(g_ref[0, (2 * direction + 1) * heads + first + slot])
        pre = _cumsum_lanes(logf)
        if direction == 0:
            bc_scr[idx] = pre
        else:
            bc_scr[idx] = jnp.sum(logf, axis=1, keepdims=True) - pre + logf
        st_scr[idx] = jnp.zeros((AUG, Dh), F32)

    def transpose_in(n, carry):
        rows = pl.ds(pl.multiple_of(n * Lc, Lc), Lc)
        for slot in range(hp):
            cols = slice(slot * Dh, (slot + 1) * Dh)
            qt_scr[slot, n] = q_ref[0, rows, cols].astype(F32).T.astype(BF16)
            vt_scr[slot, n, 0:Dh, :] = v_ref[0, rows, cols].astype(F32).T.astype(BF16)
            vt_scr[slot, n, Dh:AUG, :] = ones_row
        return carry

    lax.fori_loop(0, NC, transpose_in, 0)

    def step(idx, slot, direction, n, m_st):
        cols = slice(slot * Dh, (slot + 1) * Dh)
        pos = pl.ds(pl.multiple_of(n * Lc, Lc), Lc)
        k = k_ref[0, pos, cols]
        qt = qt_scr[slot, n]
        vta = vt_scr[slot, n]
        state = st_scr[idx]
        bc_r = bc_scr[idx, pl.ds(n, 1), :]
        ig_r = g_ref[0, (2 * direction) * heads + first + slot, pl.ds(n, 1), :]
        g_tot = bc_r[:, Lc - 1:Lc] if direction == 0 else bc_r[:, 0:1]
        gmat = jnp.broadcast_to(ig_r - bc_r, (Lc, Lc)).T
        kq = _dot(k, qt)
        yield
        dmat = jnp.where(valid[direction], bc_r + gmat, -jnp.inf)
        inter = bc_r + m_st
        m_t = jnp.maximum(inter, jnp.max(dmat, axis=0, keepdims=True))
        st = (kq * (jnp.exp(dmat - m_t) * kscale)).astype(BF16)
        qte = (qt.astype(F32) * jnp.exp(inter - m_t)).astype(BF16)
        nd = _dot(jnp.concatenate([vta, state.astype(BF16)], axis=1),
                  jnp.concatenate([st, qte], axis=0))
        w_r = g_tot - bc_r + ig_r
        m_loc = jnp.max(w_r, axis=1, keepdims=True)
        e_w = jnp.exp(w_r - m_loc) * kscale
        loc = _dot((vta.astype(F32) * e_w).astype(BF16), k)
        yield
        den = nd[Dh:Dh + 1, :]
        ht_scr[direction, slot, n] = nd[0:Dh, :] / jnp.maximum(jnp.abs(den), jnp.exp(-m_t))
        m_new = jnp.maximum(g_tot + m_st, m_loc)
        st_scr[idx] = jnp.exp(g_tot + m_st - m_new) * state + jnp.exp(m_loc - m_new) * loc
        return m_new

    def body(i, carry):
        gens = [step(idx, slot, direction, i if direction == 0 else NC - 1 - i, carry[idx])
                for idx, (slot, direction) in enumerate(chains)]
        out = [None] * len(gens)
        while any(o is None for o in out):
            for idx, gen in enumerate(gens):
                if out[idx] is None:
                    try:
                        next(gen)
                    except StopIteration as done:
                        out[idx] = done.value
        return tuple(out)

    lax.fori_loop(0, NC, body, tuple(jnp.zeros((1, 1), F32) for _ in chains))

    hn_t = [jnp.broadcast_to(hn_ref[:, slot * Dh:(slot + 1) * Dh], (Lc, Dh)).T for slot in range(hp)]

    def combine(n, carry):
        pos = pl.ds(pl.multiple_of(n * Lc, Lc), Lc)
        for slot in range(hp):
            cols = slice(slot * Dh, (slot + 1) * Dh)
            hs = ht_scr[0, slot, n] + ht_scr[1, slot, n]
            hs = hs * lax.rsqrt(jnp.mean(hs * hs, axis=0, keepdims=True) + EPS) * hn_t[slot]
            y_ref[0, pos, cols] = (jax.nn.sigmoid(o_ref[0, pos, cols].astype(F32)) * hs.T).astype(BF16)
        return carry

    lax.fori_loop(0, NC, combine, 0)


def _mlstm(proj3, gate_rows, head_norm, C, heads):
    B, S, _ = proj3.shape
    Dh = C // heads
    NC = S // ML_CHUNK
    assert Dh == LANES, "the head dimension must fill one lane tile"
    hp = 2 if heads % 2 == 0 else 1
    aug = Dh + 16
    W = hp * Dh
    base = 3 * C // W
    col = lambda off: (lambda b, h: (b, 0, base + off * (heads // hp) + h))
    return pl.pallas_call(
        functools.partial(_mlstm_kernel, heads=heads, hp=hp),
        grid=(B, heads // hp),
        in_specs=[
            pl.BlockSpec((1, S, W), col(0)),
            pl.BlockSpec((1, S, W), col(1)),
            pl.BlockSpec((1, S, W), col(2)),
            pl.BlockSpec((1, S, W), col(3)),
            pl.BlockSpec((1, 4 * heads, NC, ML_CHUNK), lambda b, h: (b, 0, 0, 0)),
            pl.BlockSpec((1, W), lambda b, h: (0, h)),
        ],
        out_specs=pl.BlockSpec((1, S, W), lambda b, h: (b, 0, h)),
        out_shape=jax.ShapeDtypeStruct((B, S, C), BF16),
        scratch_shapes=[pltpu.VMEM((hp, NC, Dh, ML_CHUNK), BF16),
                        pltpu.VMEM((hp, NC, aug, ML_CHUNK), BF16),
                        pltpu.VMEM((2, hp, NC, Dh, ML_CHUNK), F32),
                        pltpu.VMEM((2 * hp, NC, ML_CHUNK), F32),
                        pltpu.VMEM((2 * hp, aug, Dh), F32)],
        compiler_params=_params(("parallel", "parallel"), 48),
        name="mlstm",
    )(proj3, proj3, proj3, proj3, gate_rows, head_norm)


def _cmul(ar, ai, br, bi):
    return ar * br - ai * bi, ar * bi + ai * br


def _slab_interleave(re, im, axis):
    axis = axis % re.ndim
    n = re.shape[axis]
    shp = re.shape[:axis] + (n // LANES, 1, LANES) + re.shape[axis + 1:]
    out = jnp.concatenate([re.reshape(shp), im.reshape(shp)], axis=axis + 1)
    return out.reshape(re.shape[:axis] + (2 * n,) + re.shape[axis + 1:])


def _s5_kernel(u_ref, w_ref, cm_ref, e_ref, dm_ref, lam_ref, d_ref, y_ref, yf_scr, yb_scr):
    S = yf_scr.shape[0]
    T = e_ref.shape[2]
    P2 = e_ref.shape[3]
    SL = 2 * LANES
    NCH = S // T
    ri = lax.broadcasted_iota(jnp.int32, (T, T), 0)
    ci = lax.broadcasted_iota(jnp.int32, (T, T), 1)
    tris = (jnp.where(ri >= ci, 1.0, 0.0).astype(BF16), jnp.where(ri <= ci, 1.0, 0.0).astype(BF16))
    outs = (yf_scr, yb_scr)

    UN = next(u for u in (S5_UNROLL, 2, 1) if NCH % u == 0)
    chains = [(d, k, c0) for d in (0, 1) for k in range(UN) for c0 in range(0, P2, SL)]

    def body(i, carry):
        rows = {}
        for k in range(UN):
            rows[0, k] = pl.ds(pl.multiple_of((i * UN + k) * T, T), T)
            rows[1, k] = pl.ds(pl.multiple_of((NCH - 1 - i * UN - k) * T, T), T)
        u = {key: u_ref[0, r, :] for key, r in rows.items()}
        x = [_dot(u[d, k], w_ref[d, 0, :, c0:c0 + SL]) for d, k, c0 in chains]
        xs = []
        for (d, k, c0), xv in zip(chains, x):
            sr, si = _cmul(e_ref[d, 0, :, c0:c0 + LANES], e_ref[d, 0, :, c0 + LANES:c0 + SL],
                           xv[:, :LANES], xv[:, LANES:])
            xs.append(jnp.concatenate([sr, si], axis=1).astype(BF16))
        cs = [_dot(tris[d], v) for (d, k, c0), v in zip(chains, xs)]
        hs = {}
        state = {(d, c0): carry[d][:, c0:c0 + SL] for d in (0, 1) for c0 in range(0, P2, SL)}
        for (d, k, c0), cv in zip(chains, cs):
            re = slice(c0, c0 + LANES)
            im = slice(c0 + LANES, c0 + SL)
            st = state[d, c0]
            br, bi = _cmul(lam_ref[d, 0, :, re], lam_ref[d, 0, :, im],
                           st[:, :LANES], st[:, LANES:])
            hr, hi = _cmul(dm_ref[d, 0, :, re], dm_ref[d, 0, :, im],
                           cv[:, :LANES] + br, cv[:, LANES:] + bi)
            h = jnp.concatenate([hr, hi], axis=1)
            last = T - 1 if d == 0 else 0
            state[d, c0] = h[last:last + 1, :]
            hs[d, k, c0] = h
        ys = {key: _dot(h.astype(BF16), cm_ref[0, key[2]:key[2] + SL, :]) for key, h in hs.items()}
        for (d, k), r in rows.items():
            outs[d][r, :] = sum(ys[d, k, c0] for c0 in range(0, P2, SL))
        return tuple(jnp.concatenate([state[d, c0] for c0 in range(0, P2, SL)], axis=1) for d in (0, 1))

    zero = jnp.zeros((1, P2), F32)
    lax.fori_loop(0, NCH // UN, body, (zero, zero))

    R = min(512, S)
    d = d_ref[...]

    def skip_gelu(i, carry):
        r = pl.ds(pl.multiple_of(i * R, R), R)
        y = yf_scr[r, :] + yb_scr[r, :] + d * u_ref[0, r, :].astype(F32)
        y_ref[0, r, :] = jax.nn.gelu(y).astype(BF16)
        return carry

    lax.fori_loop(0, S // R, skip_gelu, 0)


def _s5(proj3, w_bd, c_bd, e_tab, d_tab, lam_tab, d_skip, C):
    B, S, _ = proj3.shape
    NB, P2, U = c_bd.shape
    T = e_tab.shape[2]
    base = 7 * C // U
    return pl.pallas_call(
        _s5_kernel,
        grid=(B, NB),
        in_specs=[
            pl.BlockSpec((1, S, U), lambda b, j: (b, 0, base + j)),
            pl.BlockSpec((2, 1, U, P2), lambda b, j: (0, j, 0, 0)),
            pl.BlockSpec((1, P2, U), lambda b, j: (j, 0, 0)),
            pl.BlockSpec((2, 1, T, P2), lambda b, j: (0, j, 0, 0)),
            pl.BlockSpec((2, 1, T, P2), lambda b, j: (0, j, 0, 0)),
            pl.BlockSpec((2, 1, 1, P2), lambda b, j: (0, j, 0, 0)),
            pl.BlockSpec((1, U), lambda b, j: (0, j)),
        ],
        out_specs=pl.BlockSpec((1, S, U), lambda b, j: (b, 0, j)),
        out_shape=jax.ShapeDtypeStruct((B, S, C), BF16),
        scratch_shapes=[pltpu.VMEM((S, U), F32), pltpu.VMEM((S, U), F32)],
        compiler_params=_params(("parallel", "parallel"), 32),
        name="s5_scan",
    )(proj3, w_bd, c_bd, e_tab, d_tab, lam_tab, d_skip)


def _s5_tables(lam_re, lam_im, log_step, b_re, b_im, c_re, c_im):
    _, G, P = lam_re.shape
    Hg = b_re.shape[-1]
    gb = S5_BLOCK_GROUPS
    NB = G // gb
    T = S5_CHUNK
    lre = jnp.minimum(lam_re, -1e-4)
    lim = lam_im
    step = jnp.exp(log_step)[..., None]
    mag = jnp.exp(lre * step)
    ang = lim * step
    lbr = mag * jnp.cos(ang)
    lbi = mag * jnp.sin(ang)
    den = lre * lre + lim * lim
    qr = ((lbr - 1.0) * lre + lbi * lim) / den
    qi = (lbi * lre - (lbr - 1.0) * lim) / den
    bbr = qr[..., None] * b_re[None] - qi[..., None] * b_im[None]
    bbi = qr[..., None] * b_im[None] + qi[..., None] * b_re[None]
    eye = jnp.eye(gb, dtype=F32)

    def drive(bb):
        t = bb.reshape(2, NB, gb, P, Hg).transpose(0, 1, 2, 4, 3)
        return jnp.einsum("ab,djahp->djahbp", eye, t).reshape(2, NB, gb * Hg, gb * P)

    w_bd = _slab_interleave(drive(bbr), drive(bbi), -1).astype(BF16)

    def readout(cc):
        t = cc.reshape(NB, gb, Hg, P).transpose(0, 1, 3, 2)
        return jnp.einsum("ab,japh->japbh", eye, t).reshape(NB, gb * P, gb * Hg)

    c_bd = _slab_interleave(readout(c_re), -readout(c_im), 1).astype(BF16)

    def blocks(a):
        return a.reshape(2, -1, NB, gb * P).transpose(0, 2, 1, 3)

    r = jnp.arange(T, dtype=F32)
    r = jnp.stack([r, T - 1.0 - r])[:, :, None, None]
    lm = (lre * step)[:, None]
    an = ang[:, None]
    e_tab = _slab_interleave(blocks(jnp.exp(-r * lm) * jnp.cos(-r * an)),
                             blocks(jnp.exp(-r * lm) * jnp.sin(-r * an)), -1)
    d_tab = _slab_interleave(blocks(jnp.exp(r * lm) * jnp.cos(r * an)),
                             blocks(jnp.exp(r * lm) * jnp.sin(r * an)), -1)
    lam_tab = _slab_interleave(blocks(lbr[:, None]), blocks(lbi[:, None]), -1)
    return w_bd, c_bd, e_tab, d_tab, lam_tab


def _merge_kernel(ya_ref, yb_ref, yc_ref, gl_ref, gw_ref, gb_ref, wg_ref, bg_ref, wb_ref,
                  o_ref, yc_scr):
    @pl.when(pl.program_id(1) == 0)
    def _():
        y = yc_ref[...]
        gate = jax.nn.sigmoid(_dot(y, gw_ref[0]) + gb_ref[...])
        yc_scr[...] = (y.astype(F32) * gate).astype(BF16)

    gl = gl_ref[...].astype(BF16)
    acc = None
    for n, br in enumerate((ya_ref[...], yb_ref[...], yc_scr[...])):
        term = jax.nn.sigmoid(_dot(gl, wg_ref[0, n]) + bg_ref[n]) * _dot(br, wb_ref[0, n])
        acc = term if acc is None else acc + term
    o_ref[...] = acc.astype(BF16)


def _merge(ya, yb, yc, small, glu_w, glu_b, wg, bg, wb, layer):
    T, C = ya.shape
    _, NBR, R, D = wg.shape
    tm = _tile(T, TOKEN_TILE)
    tn = _tile(D, 512)
    row = lambda m, n: (m, 0)
    return pl.pallas_call(
        _merge_kernel,
        grid=(T // tm, D // tn),
        in_specs=[
            pl.BlockSpec((tm, C), row),
            pl.BlockSpec((tm, C), row),
            pl.BlockSpec((tm, C), row),
            pl.BlockSpec((tm, R), row),
            pl.BlockSpec((1, C, C), lambda m, n: (layer, 0, 0)),
            pl.BlockSpec((1, C), lambda m, n: (0, 0)),
            pl.BlockSpec((1, NBR, R, tn), lambda m, n: (layer, 0, 0, n)),
            pl.BlockSpec((NBR, 1, tn), lambda m, n: (0, 0, n)),
            pl.BlockSpec((1, NBR, C, tn), lambda m, n: (layer, 0, 0, n)),
        ],
        out_specs=pl.BlockSpec((tm, tn), lambda m, n: (m, n)),
        out_shape=jax.ShapeDtypeStruct((T, D), BF16),
        scratch_shapes=[pltpu.VMEM((tm, C), BF16)],
        compiler_params=_params(("parallel", "arbitrary"), 48),
        name="branch_merge",
    )(ya, yb, yc, small, glu_w, glu_b, wg, bg, wb)


def _proj_res_kernel(a_ref, w_ref, x_ref, g_ref, o_ref):
    o_ref[...] = x_ref[...] + g_ref[0] * _dot(a_ref[...], w_ref[0])


def _proj_residual(a, w, layer, x2, gate, S, tn_target, a_buffers, name):
    T, K = a.shape
    D = w.shape[2]
    tm = _tile(S, TOKEN_TILE)
    tn = _tile(D, tn_target)
    return pl.pallas_call(
        _proj_res_kernel,
        grid=(T // tm, D // tn),
        in_specs=[
            pl.BlockSpec((tm, K), lambda m, n: (m, 0), pipeline_mode=pl.Buffered(a_buffers)),
            pl.BlockSpec((1, K, tn), lambda m, n: (layer, 0, n)),
            pl.BlockSpec((tm, tn), lambda m, n: (m, n)),
            pl.BlockSpec((1, 1, tn), lambda m, n: ((m * tm) // S, 0, n)),
        ],
        out_specs=pl.BlockSpec((tm, tn), lambda m, n: (m, n)),
        out_shape=jax.ShapeDtypeStruct((T, D), F32),
        compiler_params=_params(("parallel", "parallel"), 52),
        name=name,
    )(a, w, x2, gate)


def _ffn_up_kernel(x_ref, nw_ref, sc_ref, sh_ref, wg_ref, wu_ref, o_ref, h_scr):
    @pl.when(pl.program_id(1) == 0)
    def _():
        _norm_mod_rows(x_ref, nw_ref, sc_ref, sh_ref, h_scr)

    h = h_scr[...]
    g = _dot(h, wg_ref[0])
    o_ref[...] = (g * jax.nn.sigmoid(g) * _dot(h, wu_ref[0])).astype(BF16)


def _ffn_up(x2, S, nw, scale, shift, wg, wu, layer):
    T, D = x2.shape
    N = wg.shape[2]
    tm = _tile(S, TOKEN_TILE)
    tn = _tile(N, 512)
    bidx = lambda m, n: ((m * tm) // S, 0, 0)
    return pl.pallas_call(
        _ffn_up_kernel,
        grid=(T // tm, N // tn),
        in_specs=[
            pl.BlockSpec((tm, D), lambda m, n: (m, 0), pipeline_mode=pl.Buffered(1)),
            pl.BlockSpec((1, D), lambda m, n: (0, 0)),
            pl.BlockSpec((1, 1, D), bidx),
            pl.BlockSpec((1, 1, D), bidx),
            pl.BlockSpec((1, D, tn), lambda m, n: (layer, 0, n)),
            pl.BlockSpec((1, D, tn), lambda m, n: (layer, 0, n)),
        ],
        out_specs=pl.BlockSpec((tm, tn), lambda m, n: (m, n)),
        out_shape=jax.ShapeDtypeStruct((T, N), BF16),
        scratch_shapes=[pltpu.VMEM((tm, D), BF16)],
        compiler_params=_params(("parallel", "arbitrary"), 52),
        name="swiglu_up",
    )(x2, nw, scale, shift, wg, wu)


def _final_norm_kernel(x_ref, w_ref, o_ref):
    rows = min(NORM_ROWS, x_ref.shape[0])
    w = w_ref[...]

    def body(i, carry):
        r = pl.ds(pl.multiple_of(i * rows, rows), rows)
        x = x_ref[r, :]
        o_ref[r, :] = x * lax.rsqrt(jnp.mean(x * x, axis=-1, keepdims=True) + EPS) * w
        return carry

    lax.fori_loop(0, x_ref.shape[0] // rows, body, 0)


def _final_norm(x2, w):
    T, D = x2.shape
    tm = _tile(T, 256)
    return pl.pallas_call(
        _final_norm_kernel,
        grid=(T // tm,),
        in_specs=[pl.BlockSpec((tm, D), lambda m: (m, 0)), pl.BlockSpec((1, D), lambda m: (0, 0))],
        out_specs=pl.BlockSpec((tm, D), lambda m: (m, 0)),
        out_shape=jax.ShapeDtypeStruct((T, D), F32),
        compiler_params=_params(("parallel",), 40),
        name="final_norm",
    )(x2, w)


def _pad_to(a, axis, size):
    pad = [(0, 0)] * a.ndim
    pad[axis] = (0, size - a.shape[axis])
    return jnp.pad(a, pad)


def kernel(x, c, w_cond, w_mod, b_mod, norm_mix, norm_ffn, w_in, b_mgate, hy_conv_w, hy_conv_b, hy_f_w1, hy_f_b1, hy_f_w2, hy_f_b2, hy_f_w3, hy_sin_freq, hy_decay, hy_bias, ml_norm, s5_lam_re, s5_lam_im, s5_log_step, s5_b_re, s5_b_im, s5_c_re, s5_c_im, s5_d, s5_glu_w, s5_glu_b, w_gate_up, b_gate, w_branch, w_out, w_ffn_gate, w_ffn_up, w_ffn_down, norm_final):
    B, S, D = x.shape
    depth = w_in.shape[0]
    C = D // 4
    T = B * S
    n_gate = b_mgate.shape[-1]
    heads = n_gate // 4
    R = w_gate_up.shape[2]
    NS = -(-(R + n_gate) // LANES) * LANES
    NC = S // ML_CHUNK
    bands = (hy_f_w1.shape[1] - 1) // 2
    F = hy_f_w1.shape[2]
    Fp = -(-F // LANES) * LANES

    mod = _modulation(c, w_cond, w_mod, b_mod).reshape(depth, B, 6, 1, D)
    mats = _folded_dft_matrices(S)
    zpos = _hy_positions(S, bands, LANES)
    glu_w16, w_gate16, w_branch16, w_out16, w_ffg16, w_ffu16, w_ffd16 = (
        w.astype(BF16) for w in (s5_glu_w, w_gate_up, w_branch, w_out, w_ffn_gate, w_ffn_up, w_ffn_down))

    x2 = x.reshape(T, D)
    for l in range(depth):
        shift1, scale1, gate1, shift2, scale2, gate2 = (mod[l, :, i] for i in range(6))

        wl = w_in[l]
        w_main = jnp.concatenate([wl[:, :7 * C], wl[:, 7 * C + n_gate:8 * C + n_gate]], axis=1)
        w_small = jnp.concatenate([wl[:, 8 * C + n_gate:], wl[:, 7 * C:7 * C + n_gate]], axis=1)
        w_small = _pad_to(w_small, 1, NS)
        b_small = _pad_to(jnp.concatenate([jnp.zeros((R,), F32), b_mgate[l]]), 0, NS).reshape(1, NS)
        proj, small = _inproj(x2, S, norm_mix[l].reshape(1, D), scale1, shift1,
                              w_main.astype(BF16), w_small.astype(BF16), b_small)
        proj3 = proj.reshape(B, S, 8 * C)

        u3 = _hy_pre(proj3, hy_conv_w[l], hy_conv_b[l], C)
        kp, km, kn = _hy_filters(
            zpos, _pad_to(_pad_to(hy_f_w1[l], 0, LANES), 1, Fp), _pad_to(hy_f_b1[l], 0, Fp).reshape(1, Fp),
            _pad_to(_pad_to(hy_f_w2[l], 0, Fp), 1, Fp), _pad_to(hy_f_b2[l], 0, Fp).reshape(1, Fp),
            _pad_to(hy_sin_freq[l], 1, Fp), _pad_to(hy_f_w3[l], 0, Fp),
            hy_decay[l].reshape(1, -1), C)
        H = S // 2
        kp3, km3 = kp[None], km[None]
        kspec = _hy_filter_spectrum_folded(
            mats, _hy_fold(kp3, 0, _rollflip(kp3[:, H:]), 0, HY_ORDER * C),
            _hy_fold(km3, 0, _rollflip(km3[:, H:]), 0, HY_ORDER * C))
        hy_b = hy_bias[l].reshape(HY_ORDER, 1, C)
        u3r = _rollflip(u3[:, H:])
        za, za_col, zr, zr_col = u3, 2 * C, u3r, 2 * C
        for o in range(HY_ORDER):
            prod = _hy_spectrum_product_folded(mats, _hy_fold(za, za_col, zr, zr_col, C), kspec, o, C)
            za, zr = _hy_inverse_gate_folded(mats, prod, u3, u3r, za, za_col, zr, zr_col, hy_b, o, C)
            za_col = zr_col = 0
        y_a = jnp.concatenate([za, _rollflip(zr)], axis=1).reshape(T, C)

        gate_rows = small[:, R:R + n_gate].reshape(B, NC, ML_CHUNK, n_gate).transpose(0, 3, 1, 2)
        y_b = _mlstm(proj3, gate_rows, ml_norm[l].reshape(1, C), C, heads).reshape(T, C)

        w_bd, c_bd, e_tab, d_tab, lam_tab = _s5_tables(
            s5_lam_re[l], s5_lam_im[l], s5_log_step[l], s5_b_re[l], s5_b_im[l], s5_c_re[l], s5_c_im[l])
        y_c = _s5(proj3, w_bd, c_bd, e_tab, d_tab, lam_tab, s5_d[l].reshape(1, C), C).reshape(T, C)

        merged = _merge(y_a, y_b, y_c, small, glu_w16, s5_glu_b[l].reshape(1, C),
                        w_gate16, b_gate[l].reshape(-1, 1, D), w_branch16, l)
        x2 = _proj_residual(merged, w_out16, l, x2, gate1, S, 512, 2, "out_projection")

        hidden = _ffn_up(x2, S, norm_ffn[l].reshape(1, D), scale2, shift2, w_ffg16, w_ffu16, l)
        x2 = _proj_residual(hidden, w_ffd16, l, x2, gate2, S, 256, 1, "swiglu_down")

    return _final_norm(x2, norm_final.reshape(1, D)).reshape(B, S, D)
```

```python
import functools
import math

import jax
import jax.numpy as jnp
from jax import lax
from jax.experimental import pallas as pl
from jax.experimental.pallas import tpu as pltpu

F32 = jnp.float32
BF16 = jnp.bfloat16
EPS = 1e-6
HIGHEST = lax.Precision.HIGHEST

LANES = 128
HY_ORDER = 2
ML_CHUNK = 128
S5_CHUNK = 64
S5_BLOCK_GROUPS = 8
S5_UNROLL = 2
NORM_ROWS = 32
TOKEN_TILE = 1024
MIB = 1024 * 1024


def _tile(dim, target, align=LANES):
    if dim <= target:
        return dim
    t = (target // align) * align
    while t >= align:
        if dim % t == 0:
            return t
        t -= align
    return dim


def _params(sem, vmem_mib):
    return pltpu.CompilerParams(dimension_semantics=sem, vmem_limit_bytes=vmem_mib * MIB)


def _dot(a, b):
    return jnp.dot(a, b, preferred_element_type=F32)


def _dot_hi(a, b):
    return jnp.dot(a, b, preferred_element_type=F32, precision=HIGHEST)


def _mod_kernel(c_ref, wc_ref, wm_ref, bm_ref, o_ref):
    c = c_ref[...]
    cond_h = _dot_hi(c * jax.nn.sigmoid(c), wc_ref[...])
    o_ref[0] = _dot_hi(cond_h, wm_ref[0]) + bm_ref[0]


def _modulation(c, w_cond, w_mod, b_mod):
    B, D = c.shape
    depth, R, W = w_mod.shape
    rows = max(8, B)
    cp = jnp.zeros((rows, D), F32).at[:B].set(c)
    tn = _tile(W, 4096)
    out = pl.pallas_call(
        _mod_kernel,
        grid=(depth, W // tn),
        in_specs=[
            pl.BlockSpec((rows, D), lambda l, n: (0, 0)),
            pl.BlockSpec((D, R), lambda l, n: (0, 0)),
            pl.BlockSpec((1, R, tn), lambda l, n: (l, 0, n)),
            pl.BlockSpec((1, 1, tn), lambda l, n: (l, 0, n)),
        ],
        out_specs=pl.BlockSpec((1, rows, tn), lambda l, n: (l, 0, n)),
        out_shape=jax.ShapeDtypeStruct((depth, rows, W), F32),
        compiler_params=_params(("arbitrary", "arbitrary"), 40),
        name="adaln_modulation",
    )(cp, w_cond, w_mod, b_mod.reshape(depth, 1, W))
    return out[:, :B]


def _norm_mod(x, nw, scale, shift):
    ms = jnp.mean(x * x, axis=-1, keepdims=True)
    y = x * lax.rsqrt(ms + EPS) * nw
    return y * (1.0 + scale) + shift


def _norm_mod_rows(x_ref, nw_ref, sc_ref, sh_ref, h_scr):
    rows = min(NORM_ROWS, x_ref.shape[0])
    nw = nw_ref[...]
    scale = sc_ref[0]
    shift = sh_ref[0]

    def body(i, carry):
        r = pl.ds(pl.multiple_of(i * rows, rows), rows)
        h_scr[r, :] = _norm_mod(x_ref[r, :], nw, scale, shift).astype(BF16)
        return carry

    lax.fori_loop(0, x_ref.shape[0] // rows, body, 0)


def _inproj_kernel(x_ref, nw_ref, sc_ref, sh_ref, w_ref, ws_ref, bs_ref, o_ref, os_ref, h_scr):
    @pl.when(pl.program_id(1) == 0)
    def _():
        _norm_mod_rows(x_ref, nw_ref, sc_ref, sh_ref, h_scr)
        os_ref[...] = _dot(h_scr[...], ws_ref[...]) + bs_ref[...]

    o_ref[...] = _dot(h_scr[...], w_ref[...]).astype(BF16)


def _inproj(x2, S, nw, scale, shift, w_main, w_small, b_small):
    T, D = x2.shape
    N = w_main.shape[1]
    NS = w_small.shape[1]
    tm = _tile(S, TOKEN_TILE)
    tn = _tile(N, 512)
    bidx = lambda m, n: ((m * tm) // S, 0, 0)
    return pl.pallas_call(
        _inproj_kernel,
        grid=(T // tm, N // tn),
        in_specs=[
            pl.BlockSpec((tm, D), lambda m, n: (m, 0), pipeline_mode=pl.Buffered(1)),
            pl.BlockSpec((1, D), lambda m, n: (0, 0)),
            pl.BlockSpec((1, 1, D), bidx),
            pl.BlockSpec((1, 1, D), bidx),
            pl.BlockSpec((D, tn), lambda m, n: (0, n)),
            pl.BlockSpec((D, NS), lambda m, n: (0, 0)),
            pl.BlockSpec((1, NS), lambda m, n: (0, 0)),
        ],
        out_specs=[
            pl.BlockSpec((tm, tn), lambda m, n: (m, n)),
            pl.BlockSpec((tm, NS), lambda m, n: (m, 0)),
        ],
        out_shape=[
            jax.ShapeDtypeStruct((T, N), BF16),
            jax.ShapeDtypeStruct((T, NS), F32),
        ],
        scratch_shapes=[pltpu.VMEM((tm, D), BF16)],
        compiler_params=_params(("parallel", "arbitrary"), 56),
        name="in_projection",
    )(x2, nw, scale, shift, w_main, w_small, b_small)


def _hy_pre_kernel(u_ref, w_ref, b_ref, o_ref):
    S, tn = u_ref.shape[1], u_ref.shape[2]
    R = min(256, S)
    halo = 16
    nchunks = S // R
    w = w_ref[...]
    b = b_ref[...]
    row = lax.broadcasted_iota(jnp.int32, (R, tn), 0)

    def body(i, carry):
        r0 = pl.multiple_of(i * R, R)
        x = u_ref[0, pl.ds(r0, R), :].astype(F32)
        lo = pl.multiple_of(jnp.maximum(r0 - halo, 0), halo)
        hi = pl.multiple_of(jnp.minimum(r0 + R, S - halo), halo)
        before = u_ref[0, pl.ds(lo, halo), :].astype(F32)[halo - 1:halo]
        after = u_ref[0, pl.ds(hi, halo), :].astype(F32)[0:1]
        before = jnp.where(i == 0, 0.0, before)
        after = jnp.where(i == nchunks - 1, 0.0, after)
        prev = jnp.where(row == 0, before, pltpu.roll(x, 1, 0))
        nxt = jnp.where(row == R - 1, after, pltpu.roll(x, R - 1, 0))
        y = prev * w[0:1] + x * w[1:2] + nxt * w[2:3] + b
        o_ref[0, pl.ds(r0, R), :] = y.astype(BF16)
        return carry

    lax.fori_loop(0, nchunks, body, 0)


def _hy_pre(proj3, conv_w, conv_b, C):
    B, S, _ = proj3.shape
    W = 3 * C
    tn = _tile(W, 512)
    return pl.pallas_call(
        _hy_pre_kernel,
        grid=(B, W // tn),
        in_specs=[
            pl.BlockSpec((1, S, tn), lambda b, j: (b, 0, j)),
            pl.BlockSpec((3, tn), lambda b, j: (0, j)),
            pl.BlockSpec((1, tn), lambda b, j: (0, j)),
        ],
        out_specs=pl.BlockSpec((1, S, tn), lambda b, j: (b, 0, j)),
        out_shape=jax.ShapeDtypeStruct((B, S, W), BF16),
        compiler_params=_params(("parallel", "parallel"), 48),
        name="hyena_depthwise_conv",
    )(proj3, conv_w, conv_b.reshape(1, W))


def _hy_filter_kernel(z_ref, w1_ref, b1_ref, w2_ref, b2_ref, sf_ref, w3f_ref, w3b_ref,
                      df_ref, db_ref, kp_ref, km_ref, kn_ref, ff_scr, fb_scr, h_scr):
    L, tc = ff_scr.shape
    R = min(256, L)
    sf = sf_ref[...]
    dec_f = jnp.abs(df_ref[...])
    dec_b = jnp.abs(db_ref[...])
    row = lax.broadcasted_iota(jnp.int32, (R, tc), 0)
    sign = jnp.where((row & 1) == 0, 1.0, -1.0)

    @pl.when(jnp.logical_and(pl.program_id(0) == 0, pl.program_id(1) == 0))
    def _():
        def hidden(i, carry):
            r = pl.ds(pl.multiple_of(i * R, R), R)
            h = jnp.sin(sf[0:1] * (_dot_hi(z_ref[r, :], w1_ref[...]) + b1_ref[...]))
            h_scr[r, :] = jnp.sin(sf[1:2] * (_dot_hi(h, w2_ref[...]) + b2_ref[...]))
            return carry

        lax.fori_loop(0, L // R, hidden, 0)

    def windowed(i, carry):
        ss, kn = carry
        r = pl.ds(pl.multiple_of(i * R, R), R)
        t_unit = z_ref[r, 0:1]
        h = h_scr[r, :]
        ff = _dot_hi(h, w3f_ref[...]) * jnp.exp(-t_unit * dec_f)
        fb = _dot_hi(h, w3b_ref[...]) * jnp.exp(-t_unit * dec_b)
        fb = jnp.where(jnp.logical_and(row == 0, i == 0), 0.0, fb)
        ff_scr[r, :] = ff
        fb_scr[r, :] = fb
        ss = ss + jnp.sum(ff * ff, axis=0, keepdims=True) + jnp.sum(fb * fb, axis=0, keepdims=True)
        kn = kn + jnp.sum((ff + fb) * sign, axis=0, keepdims=True)
        return ss, kn

    zero = jnp.zeros((1, tc), F32)
    ss, kn = lax.fori_loop(0, L // R, windowed, (zero, zero))
    scale = lax.rsqrt(ss + EPS)
    kn_ref[...] = kn * scale

    def normalised(i, carry):
        r = pl.ds(pl.multiple_of(i * R, R), R)
        ff = ff_scr[r, :]
        fb = fb_scr[r, :]
        kp_ref[r, :] = ((ff + fb) * scale).astype(BF16)
        km_ref[r, :] = ((fb - ff) * scale).astype(BF16)
        return carry

    lax.fori_loop(0, L // R, normalised, 0)


def _hy_filters(zpos, w1, b1, w2, b2, sf, w3, decay, C):
    L, E = zpos.shape
    F = w1.shape[1]
    tc = _tile(C, 256)
    nt = C // tc
    fwd = lambda o, j: (0, o * nt + j)
    bwd = lambda o, j: (0, (HY_ORDER + o) * nt + j)
    const = lambda o, j: (0, 0)
    return pl.pallas_call(
        _hy_filter_kernel,
        grid=(HY_ORDER, nt),
        in_specs=[
            pl.BlockSpec((L, E), const),
            pl.BlockSpec((E, F), const),
            pl.BlockSpec((1, F), const),
            pl.BlockSpec((F, F), const),
            pl.BlockSpec((1, F), const),
            pl.BlockSpec((2, F), const),
            pl.BlockSpec((F, tc), fwd),
            pl.BlockSpec((F, tc), bwd),
            pl.BlockSpec((1, tc), fwd),
            pl.BlockSpec((1, tc), bwd),
        ],
        out_specs=[
            pl.BlockSpec((L, tc), fwd),
            pl.BlockSpec((L, tc), fwd),
            pl.BlockSpec((1, tc), fwd),
        ],
        out_shape=[
            jax.ShapeDtypeStruct((L, HY_ORDER * C), BF16),
            jax.ShapeDtypeStruct((L, HY_ORDER * C), BF16),
            jax.ShapeDtypeStruct((1, HY_ORDER * C), F32),
        ],
        scratch_shapes=[pltpu.VMEM((L, tc), F32), pltpu.VMEM((L, tc), F32), pltpu.VMEM((L, F), F32)],
        compiler_params=_params(("arbitrary", "arbitrary"), 40),
        name="hyena_filters",
    )(zpos, w1, b1, w2, b2, sf, w3, w3, decay, decay)


def _mirror_kernel(lo_ref, hi_ref, o_ref):
    R, tn = o_ref.shape[1], o_ref.shape[2]
    ri = lax.broadcasted_iota(jnp.int32, (R, R), 0)
    ci = lax.broadcasted_iota(jnp.int32, (R, R), 1)
    rev = jnp.where(jnp.logical_and(ri + ci == R, ri >= 1), 1.0, 0.0).astype(BF16)
    out = _dot(rev, lo_ref[0])
    row = lax.broadcasted_iota(jnp.int32, (R, tn), 0)
    o_ref[0] = jnp.where(row == 0, hi_ref[0, 0:1, :].astype(F32), out).astype(BF16)


def _mirror(src, base, n, W):
    B = src.shape[0]
    R = _tile(n, 256, 16)
    tn = _tile(W, 512)
    top = (base + n) // R
    return pl.pallas_call(
        _mirror_kernel,
        grid=(B, n // R, W // tn),
        in_specs=[
            pl.BlockSpec((1, R, tn), lambda b, i, j: (b, top - 1 - i, j)),
            pl.BlockSpec((1, R, tn), lambda b, i, j: (b, jnp.where(i == 0, base // R, top - i), j)),
        ],
        out_specs=pl.BlockSpec((1, R, tn), lambda b, i, j: (b, i, j)),
        out_shape=jax.ShapeDtypeStruct((B, n, W), BF16),
        compiler_params=_params(("parallel", "parallel", "parallel"), 32),
        name="hyena_mirror",
    )(src, src)


def _fold_kernel(a_ref, r_ref, p_ref, m_ref, mid_ref, nyq_ref):
    H, tn = p_ref.shape[1], p_ref.shape[2]
    R = min(256, H)
    row = lax.broadcasted_iota(jnp.int32, (R, tn), 0)
    sign = jnp.where((row & 1) == 0, 1.0, -1.0)

    def body(i, nyq):
        rows = pl.ds(pl.multiple_of(i * R, R), R)
        a = a_ref[0, rows, :].astype(F32)
        rr = r_ref[0, rows, :].astype(F32)
        first = jnp.logical_and(row == 0, i == 0)
        p = jnp.where(first, a, a + rr)
        p_ref[0, rows, :] = p.astype(BF16)
        m_ref[0, rows, :] = jnp.where(first, a, a - rr).astype(BF16)
        return nyq + jnp.sum(p * sign, axis=0, keepdims=True)

    nyq = lax.fori_loop(0, H // R, body, jnp.zeros((1, tn), F32))
    mid = r_ref[0, 0:1, :].astype(F32)
    mid_ref[0] = mid
    nyq_ref[0] = nyq + (1.0 if H % 2 == 0 else -1.0) * mid


def _hy_fold(a_arr, a_col, r_arr, r_col, W):
    B, H, _ = r_arr.shape
    tn = _tile(W, 512)
    aoff, roff = a_col // tn, r_col // tn
    half = pl.BlockSpec((1, H, tn), lambda b, j: (b, 0, j))
    rowv = pl.BlockSpec((1, 1, tn), lambda b, j: (b, 0, j))
    return pl.pallas_call(
        _fold_kernel,
        grid=(B, W // tn),
        in_specs=[pl.BlockSpec((1, H, tn), lambda b, j: (b, 0, aoff + j)),
                  pl.BlockSpec((1, H, tn), lambda b, j: (b, 0, roff + j))],
        out_specs=[half, half, rowv, rowv],
        out_shape=[jax.ShapeDtypeStruct((B, H, W), BF16)] * 2 + [jax.ShapeDtypeStruct((B, 1, W), F32)] * 2,
        compiler_params=_params(("parallel", "parallel"), 40),
        name="hyena_fold",
    )(a_arr, r_arr)


def _half_spectra(ce_ref, se_ref, co_ref, so_ref, cp, cm_, cmid, sp, sm_, smid, nyq):
    tk = ce_ref.shape[0]
    row = pl.program_id(0) * tk + lax.broadcasted_iota(jnp.int32, (tk, cp.shape[1]), 0)
    sgn = jnp.where((row & 1) == 0, 1.0, -1.0)
    re_e = _dot(ce_ref[...], cp) + sgn * cmid
    re_o = _dot(co_ref[...], cm_)
    si_e = jnp.where(row == 0, nyq, _dot(se_ref[...], sm_))
    si_o = _dot(so_ref[...], sp) + sgn * smid
    return re_e, si_e, re_o, si_o, row == 0


def _fspec_filter_kernel(ce_ref, se_ref, co_ref, so_ref, pp_ref, pm_ref, pmid_ref, pnyq_ref,
                         mp_ref, mm_ref, mmid_ref, kre_ref, kie_ref, kro_ref, kio_ref):
    re_e, si_e, re_o, si_o, _ = _half_spectra(
        ce_ref, se_ref, co_ref, so_ref, pp_ref[0], pm_ref[0], pmid_ref[0],
        mp_ref[0], mm_ref[0], mmid_ref[0], pnyq_ref[0])
    kre_ref[...] = re_e
    kie_ref[...] = si_e
    kro_ref[...] = re_o
    kio_ref[...] = si_o


def _hy_filter_spectrum_folded(mats, pfold, mfold):
    ce, se, co, so = mats[:4]
    pp, pm, pmid, pnyq = pfold
    mp, mm, mmid, _ = mfold
    _, H, W = pp.shape
    tk = _tile(H, 512)
    tn = _tile(W, 512)
    mat = pl.BlockSpec((tk, H), lambda k, j: (k, 0))
    half = pl.BlockSpec((1, H, tn), lambda k, j: (0, 0, j))
    rowv = pl.BlockSpec((1, 1, tn), lambda k, j: (0, 0, j))
    out = pl.BlockSpec((tk, tn), lambda k, j: (k, j))
    return pl.pallas_call(
        _fspec_filter_kernel,
        grid=(H // tk, W // tn),
        in_specs=[mat] * 4 + [half, half, rowv, rowv, half, half, rowv],
        out_specs=[out] * 4,
        out_shape=[jax.ShapeDtypeStruct((H, W), F32)] * 4,
        compiler_params=_params(("parallel", "parallel"), 52),
        name="hyena_filter_spectrum",
    )(ce, se, co, so, pp, pm, pmid, pnyq, mp, mm, mmid)


def _fspec_data_kernel(ce_ref, se_ref, co_ref, so_ref, zp_ref, zm_ref, mid_ref, nyq_ref,
                       kre_ref, kie_ref, kro_ref, kio_ref, pre_ref, pie_ref, pro_ref, pio_ref):
    zp, zm, mid = zp_ref[0], zm_ref[0], mid_ref[0]
    zre, zse, zro, zso, first = _half_spectra(ce_ref, se_ref, co_ref, so_ref,
                                              zp, zm, mid, zp, zm, mid, nyq_ref[0])
    kre, kie, kro, kio = kre_ref[...], kie_ref[...], kro_ref[...], kio_ref[...]
    t = zse * kie
    pre_ref[0] = (zre * kre + jnp.where(first, 0.0, t)).astype(BF16)
    pie_ref[0] = jnp.where(first, t, zre * kie - zse * kre).astype(BF16)
    pro_ref[0] = (zro * kro + zso * kio).astype(BF16)
    pio_ref[0] = (zro * kio - zso * kro).astype(BF16)


def _hy_spectrum_product_folded(mats, zfold, kspec, order, C):
    ce, se, co, so = mats[:4]
    zp, zm, mid, nyq = zfold
    B, H, _ = zp.shape
    tk = _tile(H, 512)
    tn = _tile(C, 512)
    nt = C // tn
    mat = pl.BlockSpec((tk, H), lambda k, b, j: (k, 0))
    half = pl.BlockSpec((1, H, tn), lambda k, b, j: (b, 0, j))
    rowv = pl.BlockSpec((1, 1, tn), lambda k, b, j: (b, 0, j))
    kblk = pl.BlockSpec((tk, tn), lambda k, b, j: (k, order * nt + j))
    out = pl.BlockSpec((1, tk, tn), lambda k, b, j: (b, k, j))
    return pl.pallas_call(
        _fspec_data_kernel,
        grid=(H // tk, B, nt),
        in_specs=[mat] * 4 + [half, half, rowv, rowv] + [kblk] * 4,
        out_specs=[out] * 4,
        out_shape=[jax.ShapeDtypeStruct((B, H, C), BF16)] * 4,
        compiler_params=_params(("parallel", "parallel", "parallel"), 52),
        name="hyena_spectrum_product",
    )(ce, se, co, so, zp, zm, mid, nyq, *kspec)


def _finv_kernel(ice_ref, ise_ref, ico_ref, iso_ref, pre_ref, pie_ref, pro_ref, pio_ref,
                 xa_ref, xr_ref, za_ref, zr_ref, b_ref, oa_ref, or_ref, mid_scr, *, n):
    tt, H = ice_ref.shape
    tn = oa_ref.shape[2]
    tb = pl.program_id(0)
    pnyq = pie_ref[0, 0:1, :].astype(F32)

    @pl.when(tb == 0)
    def _():
        R = min(256, H)
        row = lax.broadcasted_iota(jnp.int32, (R, tn), 0)
        sign = jnp.where((row & 1) == 0, 1.0, -1.0)

        def body(i, acc):
            rows = pl.ds(pl.multiple_of(i * R, R), R)
            d = pre_ref[0, rows, :].astype(F32) - pio_ref[0, rows, :].astype(F32)
            return acc + jnp.sum(d * sign, axis=0, keepdims=True)

        acc = lax.fori_loop(0, H // R, body, jnp.zeros((1, tn), F32))
        dc = pre_ref[0, 0:1, :].astype(F32)
        mid_scr[...] = (2.0 * acc - dc + (1.0 if H % 2 == 0 else -1.0) * pnyq) * (1.0 / n)

    ec = _dot(ice_ref[...], pre_ref[0])
    es = _dot(ise_ref[...], pie_ref[0])
    oc = _dot(ico_ref[...], pro_ref[0])
    os_ = _dot(iso_ref[...], pio_ref[0])
    row = tb * tt + lax.broadcasted_iota(jnp.int32, (tt, tn), 0)
    nyq_t = jnp.where((row & 1) == 0, 1.0 / n, -1.0 / n) * pnyq
    y1 = ec - es + oc - os_ + nyq_t
    y2 = jnp.where(row == 0, mid_scr[...], ec + es - oc - os_ + nyq_t)
    bias = b_ref[0]
    oa_ref[0] = (xa_ref[0].astype(F32) * (y1 + za_ref[0].astype(F32) * bias)).astype(BF16)
    or_ref[0] = (xr_ref[0].astype(F32) * (y2 + zr_ref[0].astype(F32) * bias)).astype(BF16)


def _hy_inverse_gate_folded(mats, prod, u3, u3r, za_arr, za_col, zr_arr, zr_col, bias, order, C):
    ice, ise, ico, iso = mats[4:]
    B, H, _ = prod[0].shape
    tt = _tile(H, 512)
    tn = _tile(C, 512)
    nt = C // tn
    aoff, roff = za_col // tn, zr_col // tn
    mat = pl.BlockSpec((tt, H), lambda t, b, j: (t, 0))
    half = pl.BlockSpec((1, H, tn), lambda t, b, j: (b, 0, j))
    gate = pl.BlockSpec((1, tt, tn), lambda t, b, j: (b, t, order * nt + j))
    out = pl.BlockSpec((1, tt, tn), lambda t, b, j: (b, t, j))
    return pl.pallas_call(
        functools.partial(_finv_kernel, n=4 * H),
        grid=(H // tt, B, nt),
        in_specs=[mat] * 4 + [half] * 4 + [
            gate, gate,
            pl.BlockSpec((1, tt, tn), lambda t, b, j: (b, t, aoff + j)),
            pl.BlockSpec((1, tt, tn), lambda t, b, j: (b, t, roff + j)),
            pl.BlockSpec((1, 1, tn), lambda t, b, j: (order, 0, j)),
        ],
        out_specs=[out, out],
        out_shape=[jax.ShapeDtypeStruct((B, H, C), BF16)] * 2,
        scratch_shapes=[pltpu.VMEM((1, tn), F32)],
        compiler_params=_params(("arbitrary", "arbitrary", "arbitrary"), 52),
        name="hyena_inverse_gate",
    )(ice, ise, ico, iso, *prod, u3, u3r, za_arr, zr_arr, bias)


def _folded_dft_matrices(L):
    H = L // 2
    N = 2 * L
    tk = _tile(H, 128, 8)
    nb = H // tk
    idx = jnp.arange(H, dtype=jnp.int32)
    loc = jnp.arange(tk, dtype=jnp.int32)
    blk = jnp.arange(nb, dtype=jnp.int32) * tk

    def trig(mult_rows, mult_cols):
        ph = ((mult_rows[:, None] * mult_cols[None, :]) % N).astype(F32) * (2.0 * math.pi / N)
        return jnp.cos(ph), jnp.sin(ph)

    tables = [*trig(2 * loc, idx), *trig(2 * loc + 1, idx), *trig(loc, 2 * idx + 1)]
    bases = [a.reshape(nb, 1, H) for a in (*trig(2 * blk, idx), *trig(blk, 2 * idx + 1))]
    table = pl.BlockSpec((tk, H), lambda k: (0, 0))
    base = pl.BlockSpec((1, 1, H), lambda k: (k, 0, 0))
    out = pl.BlockSpec((tk, H), lambda k: (k, 0))
    return pl.pallas_call(
        functools.partial(_folded_dft_gen_kernel, n=N),
        grid=(nb,),
        in_specs=[table] * 6 + [base] * 4,
        out_specs=[out] * 8,
        out_shape=[jax.ShapeDtypeStruct((H, H), BF16)] * 8,
        compiler_params=_params(("parallel",), 40),
        name="dft_matrices",
    )(*tables, *bases)


def _folded_dft_gen_kernel(tec_ref, tes_ref, toc_ref, tos_ref, ttc_ref, tts_ref,
                           bec_ref, bes_ref, btc_ref, bts_ref,
                           ce_ref, se_ref, co_ref, so_ref, ice_ref, ise_ref, ico_ref, iso_ref, *, n):
    tk, H = tec_ref.shape
    rows = min(NORM_ROWS, tk)
    bec, bes, btc, bts = bec_ref[0], bes_ref[0], btc_ref[0], bts_ref[0]
    col = lax.broadcasted_iota(jnp.int32, (rows, H), 1)
    w = 2.0 / n

    def body(i, carry):
        r = pl.ds(pl.multiple_of(i * rows, rows), rows)
        tec, tes = tec_ref[r, :], tes_ref[r, :]
        toc, tos = toc_ref[r, :], tos_ref[r, :]
        ttc, tts = ttc_ref[r, :], tts_ref[r, :]
        ce = bec * tec - bes * tes
        se = bes * tec + bec * tes
        ce_ref[r, :] = ce.astype(BF16)
        se_ref[r, :] = se.astype(BF16)
        co_ref[r, :] = (bec * toc - bes * tos).astype(BF16)
        so_ref[r, :] = (bes * toc + bec * tos).astype(BF16)
        ice_ref[r, :] = jnp.where(col == 0, 1.0 / n, ce * w).astype(BF16)
        ise_ref[r, :] = (se * w).astype(BF16)
        ico_ref[r, :] = ((btc * ttc - bts * tts) * w).astype(BF16)
        iso_ref[r, :] = ((bts * ttc + btc * tts) * w).astype(BF16)
        return carry

    lax.fori_loop(0, tk // rows, body, 0)


def _hy_positions(L, bands, width):
    t = jnp.arange(L, dtype=F32)
    t_unit = t / (L - 1)
    freqs = jnp.linspace(1e-4, bands - 1, bands, dtype=F32)
    ang = (2.0 * math.pi / L) * t[:, None] * freqs[None, :]
    z = jnp.concatenate([t_unit[:, None], jnp.cos(ang), jnp.sin(ang)], axis=-1)
    return jnp.pad(z, ((0, 0), (0, width - z.shape[1])))


def _log_sigmoid(x):
    return jnp.minimum(x, 0.0) - jnp.log(1.0 + jnp.exp(-jnp.abs(x)))


def _cumsum_lanes(x):
    lane = lax.broadcasted_iota(jnp.int32, x.shape, 1)
    sh = 1
    while sh < x.shape[1]:
        x = x + jnp.where(lane >= sh, pltpu.roll(x, sh, 1), 0.0)
        sh *= 2
    return x


def _mlstm_kernel(q_ref, k_ref, v_ref, o_ref, g_ref, hn_ref, y_ref,
                  qt_scr, vt_scr, ht_scr, bc_scr, st_scr, *, heads, hp):
    NC, Dh = qt_scr.shape[1], qt_scr.shape[2]
    AUG = vt_scr.shape[2]
    Lc = ML_CHUNK
    first = pl.program_id(1) * hp
    kscale = Dh ** -0.5
    ri = lax.broadcasted_iota(jnp.int32, (Lc, Lc), 0)
    ci = lax.broadcasted_iota(jnp.int32, (Lc, Lc), 1)
    valid = (ri <= ci, ri >= ci)
    chains = [(slot, direction) for slot in range(hp) for direction in (0, 1)]
    ones_row = jnp.where(lax.broadcasted_iota(jnp.int32, (AUG - Dh, Lc), 0) == 0, 1.0, 0.0).astype(BF16)

    for idx, (slot, direction) in enumerate(chains):
        logf = _log_sigmoid(g_ref[0, (2 * direction + 1) * heads + first + slot])
        pre = _cumsum_lanes(logf)
        if direction == 0:
            bc_scr[idx] = pre
        else:
            bc_scr[idx] = jnp.sum(logf, axis=1, keepdims=True) - pre + logf
        st_scr[idx] = jnp.zeros((AUG, Dh), F32)

    def transpose_in(n, carry):
        rows = pl.ds(pl.multiple_of(n * Lc, Lc), Lc)
        for slot in range(hp):
            cols = slice(slot * Dh, (slot + 1) * Dh)
            qt_scr[slot, n] = q_ref[0, rows, cols].astype(F32).T.astype(BF16)
            vt_scr[slot, n, 0:Dh, :] = v_ref[0, rows, cols].astype(F32).T.astype(BF16)
            vt_scr[slot, n, Dh:AUG, :] = ones_row
        return carry

    lax.fori_loop(0, NC, transpose_in, 0)

    def step(idx, slot, direction, n, m_st):
        cols = slice(slot * Dh, (slot + 1) * Dh)
        pos = pl.ds(pl.multiple_of(n * Lc, Lc), Lc)
        k = k_ref[0, pos, cols]
        qt = qt_scr[slot, n]
        vta = vt_scr[slot, n]
        state = st_scr[idx]
        bc_r = bc_scr[idx, pl.ds(n, 1), :]
        ig_r = g_ref[0, (2 * direction) * heads + first + slot, pl.ds(n, 1), :]
        g_tot = bc_r[:, Lc - 1:Lc] if direction == 0 else bc_r[:, 0:1]
        gmat = jnp.broadcast_to(ig_r - bc_r, (Lc, Lc)).T
        kq = _dot(k, qt)
        yield
        dmat = jnp.where(valid[direction], bc_r + gmat, -jnp.inf)
        inter = bc_r + m_st
        m_t = jnp.maximum(inter, jnp.max(dmat, axis=0, keepdims=True))
        st = (kq * (jnp.exp(dmat - m_t) * kscale)).astype(BF16)
        qte = (qt.astype(F32) * jnp.exp(inter - m_t)).astype(BF16)
        nd = _dot(jnp.concatenate([vta, state.astype(BF16)], axis=1),
                  jnp.concatenate([st, qte], axis=0))
        w_r = g_tot - bc_r + ig_r
        m_loc = jnp.max(w_r, axis=1, keepdims=True)
        e_w = jnp.exp(w_r - m_loc) * kscale
        loc = _dot((vta.astype(F32) * e_w).astype(BF16), k)
        yield
        den = nd[Dh:Dh + 1, :]
        ht_scr[direction, slot, n] = nd[0:Dh, :] / jnp.maximum(jnp.abs(den), jnp.exp(-m_t))
        m_new = jnp.maximum(g_tot + m_st, m_loc)
        st_scr[idx] = jnp.exp(g_tot + m_st - m_new) * state + jnp.exp(m_loc - m_new) * loc
        return m_new

    def body(i, carry):
        gens = [step(idx, slot, direction, i if direction == 0 else NC - 1 - i, carry[idx])
                for idx, (slot, direction) in enumerate(chains)]
        out = [None] * len(gens)
        while any(o is None for o in out):
            for idx, gen in enumerate(gens):
                if out[idx] is None:
                    try:
                        next(gen)
                    except StopIteration as done:
                        out[idx] = done.value
        return tuple(out)

    lax.fori_loop(0, NC, body, tuple(jnp.zeros((1, 1), F32) for _ in chains))

    hn_t = [jnp.broadcast_to(hn_ref[:, slot * Dh:(slot + 1) * Dh], (Lc, Dh)).T for slot in range(hp)]

    def combine(n, carry):
        pos = pl.ds(pl.multiple_of(n * Lc, Lc), Lc)
        for slot in range(hp):
            cols = slice(slot * Dh, (slot + 1) * Dh)
            hs = ht_scr[0, slot, n] + ht_scr[1, slot, n]
            hs = hs * lax.rsqrt(jnp.mean(hs * hs, axis=0, keepdims=True) + EPS) * hn_t[slot]
            y_ref[0, pos, cols] = (jax.nn.sigmoid(o_ref[0, pos, cols].astype(F32)) * hs.T).astype(BF16)
        return carry

    lax.fori_loop(0, NC, combine, 0)


def _mlstm(proj3, gate_rows, head_norm, C, heads):
    B, S, _ = proj3.shape
    Dh = C // heads
    NC = S // ML_CHUNK
    assert Dh == LANES, "the head dimension must fill one lane tile"
    hp = 2 if heads % 2 == 0 else 1
    aug = Dh + 16
    W = hp * Dh
    base = 3 * C // W
    col = lambda off: (lambda b, h: (b, 0, base + off * (heads // hp) + h))
    return pl.pallas_call(
        functools.partial(_mlstm_kernel, heads=heads, hp=hp),
        grid=(B, heads // hp),
        in_specs=[
            pl.BlockSpec((1, S, W), col(0)),
            pl.BlockSpec((1, S, W), col(1)),
            pl.BlockSpec((1, S, W), col(2)),
            pl.BlockSpec((1, S, W), col(3)),
            pl.BlockSpec((1, 4 * heads, NC, ML_CHUNK), lambda b, h: (b, 0, 0, 0)),
            pl.BlockSpec((1, W), lambda b, h: (0, h)),
        ],
        out_specs=pl.BlockSpec((1, S, W), lambda b, h: (b, 0, h)),
        out_shape=jax.ShapeDtypeStruct((B, S, C), BF16),
        scratch_shapes=[pltpu.VMEM((hp, NC, Dh, ML_CHUNK), BF16),
                        pltpu.VMEM((hp, NC, aug, ML_CHUNK), BF16),
                        pltpu.VMEM((2, hp, NC, Dh, ML_CHUNK), F32),
                        pltpu.VMEM((2 * hp, NC, ML_CHUNK), F32),
                        pltpu.VMEM((2 * hp, aug, Dh), F32)],
        compiler_params=_params(("parallel", "parallel"), 48),
        name="mlstm",
    )(proj3, proj3, proj3, proj3, gate_rows, head_norm)


def _cmul(ar, ai, br, bi):
    return ar * br - ai * bi, ar * bi + ai * br


def _slab_interleave(re, im, axis):
    axis = axis % re.ndim
    n = re.shape[axis]
    shp = re.shape[:axis] + (n // LANES, 1, LANES) + re.shape[axis + 1:]
    out = jnp.concatenate([re.reshape(shp), im.reshape(shp)], axis=axis + 1)
    return out.reshape(re.shape[:axis] + (2 * n,) + re.shape[axis + 1:])


def _s5_kernel(u_ref, w_ref, cm_ref, e_ref, dm_ref, lam_ref, d_ref, y_ref, yf_scr, yb_scr):
    S = yf_scr.shape[0]
    T = e_ref.shape[2]
    P2 = e_ref.shape[3]
    SL = 2 * LANES
    NCH = S // T
    ri = lax.broadcasted_iota(jnp.int32, (T, T), 0)
    ci = lax.broadcasted_iota(jnp.int32, (T, T), 1)
    tris = (jnp.where(ri >= ci, 1.0, 0.0).astype(BF16), jnp.where(ri <= ci, 1.0, 0.0).astype(BF16))
    outs = (yf_scr, yb_scr)

    UN = next(u for u in (S5_UNROLL, 2, 1) if NCH % u == 0)
    chains = [(d, k, c0) for d in (0, 1) for k in range(UN) for c0 in range(0, P2, SL)]

    def body(i, carry):
        rows = {}
        for k in range(UN):
            rows[0, k] = pl.ds(pl.multiple_of((i * UN + k) * T, T), T)
            rows[1, k] = pl.ds(pl.multiple_of((NCH - 1 - i * UN - k) * T, T), T)
        u = {key: u_ref[0, r, :] for key, r in rows.items()}
        x = [_dot(u[d, k], w_ref[d, 0, :, c0:c0 + SL]) for d, k, c0 in chains]
        xs = []
        for (d, k, c0), xv in zip(chains, x):
            sr, si = _cmul(e_ref[d, 0, :, c0:c0 + LANES], e_ref[d, 0, :, c0 + LANES:c0 + SL],
                           xv[:, :LANES], xv[:, LANES:])
            xs.append(jnp.concatenate([sr, si], axis=1).astype(BF16))
        cs = [_dot(tris[d], v) for (d, k, c0), v in zip(chains, xs)]
        hs = {}
        state = {(d, c0): carry[d][:, c0:c0 + SL] for d in (0, 1) for c0 in range(0, P2, SL)}
        for (d, k, c0), cv in zip(chains, cs):
            re = slice(c0, c0 + LANES)
            im = slice(c0 + LANES, c0 + SL)
            st = state[d, c0]
            br, bi = _cmul(lam_ref[d, 0, :, re], lam_ref[d, 0, :, im],
                           st[:, :LANES], st[:, LANES:])
            hr, hi = _cmul(dm_ref[d, 0, :, re], dm_ref[d, 0, :, im],
                           cv[:, :LANES] + br, cv[:, LANES:] + bi)
            h = jnp.concatenate([hr, hi], axis=1)
            last = T - 1 if d == 0 else 0
            state[d, c0] = h[last:last + 1, :]
            hs[d, k, c0] = h
        ys = {key: _dot(h.astype(BF16), cm_ref[0, key[2]:key[2] + SL, :]) for key, h in hs.items()}
        for (d, k), r in rows.items():
            outs[d][r, :] = sum(ys[d, k, c0] for c0 in range(0, P2, SL))
        return tuple(jnp.concatenate([state[d, c0] for c0 in range(0, P2, SL)], axis=1) for d in (0, 1))

    zero = jnp.zeros((1, P2), F32)
    lax.fori_loop(0, NCH // UN, body, (zero, zero))

    R = min(512, S)
    d = d_ref[...]

    def skip_gelu(i, carry):
        r = pl.ds(pl.multiple_of(i * R, R), R)
        y = yf_scr[r, :] + yb_scr[r, :] + d * u_ref[0, r, :].astype(F32)
        y_ref[0, r, :] = jax.nn.gelu(y).astype(BF16)
        return carry

    lax.fori_loop(0, S // R, skip_gelu, 0)


def _s5(proj3, w_bd, c_bd, e_tab, d_tab, lam_tab, d_skip, C):
    B, S, _ = proj3.shape
    NB, P2, U = c_bd.shape
    T = e_tab.shape[2]
    base = 7 * C // U
    return pl.pallas_call(
        _s5_kernel,
        grid=(B, NB),
        in_specs=[
            pl.BlockSpec((1, S, U), lambda b, j: (b, 0, base + j)),
            pl.BlockSpec((2, 1, U, P2), lambda b, j: (0, j, 0, 0)),
            pl.BlockSpec((1, P2, U), lambda b, j: (j, 0, 0)),
            pl.BlockSpec((2, 1, T, P2), lambda b, j: (0, j, 0, 0)),
            pl.BlockSpec((2, 1, T, P2), lambda b, j: (0, j, 0, 0)),
            pl.BlockSpec((2, 1, 1, P2), lambda b, j: (0, j, 0, 0)),
            pl.BlockSpec((1, U), lambda b, j: (0, j)),
        ],
        out_specs=pl.BlockSpec((1, S, U), lambda b, j: (b, 0, j)),
        out_shape=jax.ShapeDtypeStruct((B, S, C), BF16),
        scratch_shapes=[pltpu.VMEM((S, U), F32), pltpu.VMEM((S, U), F32)],
        compiler_params=_params(("parallel", "parallel"), 32),
        name="s5_scan",
    )(proj3, w_bd, c_bd, e_tab, d_tab, lam_tab, d_skip)


def _s5_tables(lam_re, lam_im, log_step, b_re, b_im, c_re, c_im):
    _, G, P = lam_re.shape
    Hg = b_re.shape[-1]
    gb = S5_BLOCK_GROUPS
    NB = G // gb
    T = S5_CHUNK
    lre = jnp.minimum(lam_re, -1e-4)
    lim = lam_im
    step = jnp.exp(log_step)[..., None]
    mag = jnp.exp(lre * step)
    ang = lim * step
    lbr = mag * jnp.cos(ang)
    lbi = mag * jnp.sin(ang)
    den = lre * lre + lim * lim
    qr = ((lbr - 1.0) * lre + lbi * lim) / den
    qi = (lbi * lre - (lbr - 1.0) * lim) / den
    bbr = qr[..., None] * b_re[None] - qi[..., None] * b_im[None]
    bbi = qr[..., None] * b_im[None] + qi[..., None] * b_re[None]
    eye = jnp.eye(gb, dtype=F32)

    def drive(bb):
        t = bb.reshape(2, NB, gb, P, Hg).transpose(0, 1, 2, 4, 3)
        return jnp.einsum("ab,djahp->djahbp", eye, t).reshape(2, NB, gb * Hg, gb * P)

    w_bd = _slab_interleave(drive(bbr), drive(bbi), -1).astype(BF16)

    def readout(cc):
        t = cc.reshape(NB, gb, Hg, P).transpose(0, 1, 3, 2)
        return jnp.einsum("ab,japh->japbh", eye, t).reshape(NB, gb * P, gb * Hg)

    c_bd = _slab_interleave(readout(c_re), -readout(c_im), 1).astype(BF16)

    def blocks(a):
        return a.reshape(2, -1, NB, gb * P).transpose(0, 2, 1, 3)

    r = jnp.arange(T, dtype=F32)
    r = jnp.stack([r, T - 1.0 - r])[:, :, None, None]
    lm = (lre * step)[:, None]
    an = ang[:, None]
    e_tab = _slab_interleave(blocks(jnp.exp(-r * lm) * jnp.cos(-r * an)),
                             blocks(jnp.exp(-r * lm) * jnp.sin(-r * an)), -1)
    d_tab = _slab_interleave(blocks(jnp.exp(r * lm) * jnp.cos(r * an)),
                             blocks(jnp.exp(r * lm) * jnp.sin(r * an)), -1)
    lam_tab = _slab_interleave(blocks(lbr[:, None]), blocks(lbi[:, None]), -1)
    return w_bd, c_bd, e_tab, d_tab, lam_tab


def _merge_kernel(ya_ref, yb_ref, yc_ref, gl_ref, gw_ref, gb_ref, wg_ref, bg_ref, wb_ref,
                  o_ref, yc_scr):
    @pl.when(pl.program_id(1) == 0)
    def _():
        y = yc_ref[...]
        gate = jax.nn.sigmoid(_dot(y, gw_ref[0]) + gb_ref[...])
        yc_scr[...] = (y.astype(F32) * gate).astype(BF16)

    gl = gl_ref[...].astype(BF16)
    acc = None
    for n, br in enumerate((ya_ref[...], yb_ref[...], yc_scr[...])):
        term = jax.nn.sigmoid(_dot(gl, wg_ref[0, n]) + bg_ref[n]) * _dot(br, wb_ref[0, n])
        acc = term if acc is None else acc + term
    o_ref[...] = acc.astype(BF16)


def _merge(ya, yb, yc, small, glu_w, glu_b, wg, bg, wb, layer):
    T, C = ya.shape
    _, NBR, R, D = wg.shape
    tm = _tile(T, TOKEN_TILE)
    tn = _tile(D, 512)
    row = lambda m, n: (m, 0)
    return pl.pallas_call(
        _merge_kernel,
        grid=(T // tm, D // tn),
        in_specs=[
            pl.BlockSpec((tm, C), row),
            pl.BlockSpec((tm, C), row),
            pl.BlockSpec((tm, C), row),
            pl.BlockSpec((tm, R), row),
            pl.BlockSpec((1, C, C), lambda m, n: (layer, 0, 0)),
            pl.BlockSpec((1, C), lambda m, n: (0, 0)),
            pl.BlockSpec((1, NBR, R, tn), lambda m, n: (layer, 0, 0, n)),
            pl.BlockSpec((NBR, 1, tn), lambda m, n: (0, 0, n)),
            pl.BlockSpec((1, NBR, C, tn), lambda m, n: (layer, 0, 0, n)),
        ],
        out_specs=pl.BlockSpec((tm, tn), lambda m, n: (m, n)),
        out_shape=jax.ShapeDtypeStruct((T, D), BF16),
        scratch_shapes=[pltpu.VMEM((tm, C), BF16)],
        compiler_params=_params(("parallel", "arbitrary"), 48),
        name="branch_merge",
    )(ya, yb, yc, small, glu_w, glu_b, wg, bg, wb)


def _proj_res_kernel(a_ref, w_ref, x_ref, g_ref, o_ref):
    o_ref[...] = x_ref[...] + g_ref[0] * _dot(a_ref[...], w_ref[0])


def _proj_residual(a, w, layer, x2, gate, S, tn_target, a_buffers, name):
    T, K = a.shape
    D = w.shape[2]
    tm = _tile(S, TOKEN_TILE)
    tn = _tile(D, tn_target)
    return pl.pallas_call(
        _proj_res_kernel,
        grid=(T // tm, D // tn),
        in_specs=[
            pl.BlockSpec((tm, K), lambda m, n: (m, 0), pipeline_mode=pl.Buffered(a_buffers)),
            pl.BlockSpec((1, K, tn), lambda m, n: (layer, 0, n)),
            pl.BlockSpec((tm, tn), lambda m, n: (m, n)),
            pl.BlockSpec((1, 1, tn), lambda m, n: ((m * tm) // S, 0, n)),
        ],
        out_specs=pl.BlockSpec((tm, tn), lambda m, n: (m, n)),
        out_shape=jax.ShapeDtypeStruct((T, D), F32),
        compiler_params=_params(("parallel", "parallel"), 52),
        name=name,
    )(a, w, x2, gate)


def _ffn_up_kernel(x_ref, nw_ref, sc_ref, sh_ref, wg_ref, wu_ref, o_ref, h_scr):
    @pl.when(pl.program_id(1) == 0)
    def _():
        _norm_mod_rows(x_ref, nw_ref, sc_ref, sh_ref, h_scr)

    h = h_scr[...]
    g = _dot(h, wg_ref[0])
    o_ref[...] = (g * jax.nn.sigmoid(g) * _dot(h, wu_ref[0])).astype(BF16)


def _ffn_up(x2, S, nw, scale, shift, wg, wu, layer):
    T, D = x2.shape
    N = wg.shape[2]
    tm = _tile(S, TOKEN_TILE)
    tn = _tile(N, 512)
    bidx = lambda m, n: ((m * tm) // S, 0, 0)
    return pl.pallas_call(
        _ffn_up_kernel,
        grid=(T // tm, N // tn),
        in_specs=[
            pl.BlockSpec((tm, D), lambda m, n: (m, 0), pipeline_mode=pl.Buffered(1)),
            pl.BlockSpec((1, D), lambda m, n: (0, 0)),
            pl.BlockSpec((1, 1, D), bidx),
            pl.BlockSpec((1, 1, D), bidx),
            pl.BlockSpec((1, D, tn), lambda m, n: (layer, 0, n)),
            pl.BlockSpec((1, D, tn), lambda m, n: (layer, 0, n)),
        ],
        out_specs=pl.BlockSpec((tm, tn), lambda m, n: (m, n)),
        out_shape=jax.ShapeDtypeStruct((T, N), BF16),
        scratch_shapes=[pltpu.VMEM((tm, D), BF16)],
        compiler_params=_params(("parallel", "arbitrary"), 52),
        name="swiglu_up",
    )(x2, nw, scale, shift, wg, wu)


def _final_norm_kernel(x_ref, w_ref, o_ref):
    rows = min(NORM_ROWS, x_ref.shape[0])
    w = w_ref[...]

    def body(i, carry):
        r = pl.ds(pl.multiple_of(i * rows, rows), rows)
        x = x_ref[r, :]
        o_ref[r, :] = x * lax.rsqrt(jnp.mean(x * x, axis=-1, keepdims=True) + EPS) * w
        return carry

    lax.fori_loop(0, x_ref.shape[0] // rows, body, 0)


def _final_norm(x2, w):
    T, D = x2.shape
    tm = _tile(T, 256)
    return pl.pallas_call(
        _final_norm_kernel,
        grid=(T // tm,),
        in_specs=[pl.BlockSpec((tm, D), lambda m: (m, 0)), pl.BlockSpec((1, D), lambda m: (0, 0))],
        out_specs=pl.BlockSpec((tm, D), lambda m: (m, 0)),
        out_shape=jax.ShapeDtypeStruct((T, D), F32),
        compiler_params=_params(("parallel",), 40),
        name="final_norm",
    )(x2, w)


def _pad_to(a, axis, size):
    pad = [(0, 0)] * a.ndim
    pad[axis] = (0, size - a.shape[axis])
    return jnp.pad(a, pad)


def kernel(x, c, w_cond, w_mod, b_mod, norm_mix, norm_ffn, w_in, b_mgate, hy_conv_w, hy_conv_b, hy_f_w1, hy_f_b1, hy_f_w2, hy_f_b2, hy_f_w3, hy_sin_freq, hy_decay, hy_bias, ml_norm, s5_lam_re, s5_lam_im, s5_log_step, s5_b_re, s5_b_im, s5_c_re, s5_c_im, s5_d, s5_glu_w, s5_glu_b, w_gate_up, b_gate, w_branch, w_out, w_ffn_gate, w_ffn_up, w_ffn_down, norm_final):
    B, S, D = x.shape
    depth = w_in.shape[0]
    C = D // 4
    T = B * S
    n_gate = b_mgate.shape[-1]
    heads = n_gate // 4
    R = w_gate_up.shape[2]
    NS = -(-(R + n_gate) // LANES) * LANES
    NC = S // ML_CHUNK
    bands = (hy_f_w1.shape[1] - 1) // 2
    F = hy_f_w1.shape[2]
    Fp = -(-F // LANES) * LANES

    mod = _modulation(c, w_cond, w_mod, b_mod).reshape(depth, B, 6, 1, D)
    mats = _folded_dft_matrices(S)
    zpos = _hy_positions(S, bands, LANES)
    glu_w16, w_gate16, w_branch16, w_out16, w_ffg16, w_ffu16, w_ffd16 = (
        w.astype(BF16) for w in (s5_glu_w, w_gate_up, w_branch, w_out, w_ffn_gate, w_ffn_up, w_ffn_down))

    x2 = x.reshape(T, D)
    for l in range(depth):
        shift1, scale1, gate1, shift2, scale2, gate2 = (mod[l, :, i] for i in range(6))

        wl = w_in[l]
        w_main = jnp.concatenate([wl[:, :7 * C], wl[:, 7 * C + n_gate:8 * C + n_gate]], axis=1)
        w_small = jnp.concatenate([wl[:, 8 * C + n_gate:], wl[:, 7 * C:7 * C + n_gate]], axis=1)
        w_small = _pad_to(w_small, 1, NS)
        b_small = _pad_to(jnp.concatenate([jnp.zeros((R,), F32), b_mgate[l]]), 0, NS).reshape(1, NS)
        proj, small = _inproj(x2, S, norm_mix[l].reshape(1, D), scale1, shift1,
                              w_main.astype(BF16), w_small.astype(BF16), b_small)
        proj3 = proj.reshape(B, S, 8 * C)

        u3 = _hy_pre(proj3, hy_conv_w[l], hy_conv_b[l], C)
        kp, km, kn = _hy_filters(
            zpos, _pad_to(_pad_to(hy_f_w1[l], 0, LANES), 1, Fp), _pad_to(hy_f_b1[l], 0, Fp).reshape(1, Fp),
            _pad_to(_pad_to(hy_f_w2[l], 0, Fp), 1, Fp), _pad_to(hy_f_b2[l], 0, Fp).reshape(1, Fp),
            _pad_to(hy_sin_freq[l], 1, Fp), _pad_to(hy_f_w3[l], 0, Fp),
            hy_decay[l].reshape(1, -1), C)
        H = S // 2
        kp3, km3 = kp[None], km[None]
        kspec = _hy_filter_spectrum_folded(
            mats, _hy_fold(kp3, 0, _mirror(kp3, H, H, HY_ORDER * C), 0, HY_ORDER * C),
            _hy_fold(km3, 0, _mirror(km3, H, H, HY_ORDER * C), 0, HY_ORDER * C))
        hy_b = hy_bias[l].reshape(HY_ORDER, 1, C)
        u3r = _mirror(u3, H, H, 3 * C)
        za, za_col, zr, zr_col = u3, 2 * C, u3r, 2 * C
        for o in range(HY_ORDER):
            prod = _hy_spectrum_product_folded(mats, _hy_fold(za, za_col, zr, zr_col, C), kspec, o, C)
            za, zr = _hy_inverse_gate_folded(mats, prod, u3, u3r, za, za_col, zr, zr_col, hy_b, o, C)
            za_col = zr_col = 0
        y_a = jnp.concatenate([za, _mirror(zr, 0, H, C)], axis=1).reshape(T, C)

        gate_rows = small[:, R:R + n_gate].reshape(B, NC, ML_CHUNK, n_gate).transpose(0, 3, 1, 2)
        y_b = _mlstm(proj3, gate_rows, ml_norm[l].reshape(1, C), C, heads).reshape(T, C)

        w_bd, c_bd, e_tab, d_tab, lam_tab = _s5_tables(
            s5_lam_re[l], s5_lam_im[l], s5_log_step[l], s5_b_re[l], s5_b_im[l], s5_c_re[l], s5_c_im[l])
        y_c = _s5(proj3, w_bd, c_bd, e_tab, d_tab, lam_tab, s5_d[l].reshape(1, C), C).reshape(T, C)

        merged = _merge(y_a, y_b, y_c, small, glu_w16, s5_glu_b[l].reshape(1, C),
                        w_gate16, b_gate[l].reshape(-1, 1, D), w_branch16, l)
        x2 = _proj_residual(merged, w_out16, l, x2, gate1, S, 512, 2, "out_projection")

        hidden = _ffn_up(x2, S, norm_ffn[l].reshape(1, D), scale2, shift2, w_ffg16, w_ffu16, l)
        x2 = _proj_residual(hidden, w_ffd16, l, x2, gate2, S, 256, 1, "swiglu_down")

    return _final_norm(x2, norm_final.reshape(1, D)).reshape(B, S, D)
```
